```python
import math
import jax
import jax.numpy as jnp
from jax import lax
import numpy as np

D_MODEL = 1024
BATCH = 8
SEQ = 8192
DEPTH = 4
DEC_BATCH = 1
DEC_SEQ = 16384
PAST_LEN = 128

N_EVEN = (DEPTH + 1) // 2
N_ODD = DEPTH // 2
MIX_WIDTH = D_MODEL
HALF = MIX_WIDTH // 2
HEAD_DIM = 64

GLA_HEADS = 4
GLA_DV = HALF // GLA_HEADS
GLA_DK = GLA_DV // 2
GLA_RANK = 16
GLA_TAU = 16.0
GLA_CHUNK = 64

CONV_CH = HALF
CONV_WIDTH = 3

ATT_Q_HEADS = HALF // HEAD_DIM
ATT_KV_HEADS = 2
ATT_WINDOW = 128
ATT_BLOCK = 128

S5_CH = HALF
S5_GROUP = 16
S5_GROUPS = S5_CH // S5_GROUP
S5_STATE = 64

N_EXPERTS = 32
TOP_K = 4
D_FF = D_MODEL
SWIGLU_LIMIT = 7.0
SWIGLU_ALPHA = 1.702
MOE_BLOCK = 256

NORM_EPS = 1e-6

EVEN_SPLITS = (GLA_HEADS * GLA_DK, GLA_HEADS * GLA_DK, GLA_HEADS * GLA_DV, GLA_HEADS * GLA_DV,
               GLA_RANK, GLA_RANK, CONV_CH, CONV_CH, CONV_CH)
EVEN_IN = sum(EVEN_SPLITS)
ODD_SPLITS = (ATT_Q_HEADS * HEAD_DIM, ATT_KV_HEADS * HEAD_DIM, ATT_KV_HEADS * HEAD_DIM, S5_CH)
ODD_IN = sum(ODD_SPLITS)

kernel_name = 'hybrid_bidir_gla_conv_swa_s5_moe_encoder'


def rms_norm(x, g):
    xf = x.astype(jnp.float32)
    y = xf * lax.rsqrt(jnp.mean(xf * xf, axis=-1, keepdims=True) + NORM_EPS)
    return (y * g.astype(jnp.float32)).astype(x.dtype)


def split_cols(a, sizes):
    idx = np.cumsum(np.array(sizes))[:-1].tolist()
    return jnp.split(a, idx, axis=-1)


def gla_chunked(q, k, v, log_a):
    B, T, H, dk = q.shape
    dv = v.shape[-1]
    L = GLA_CHUNK
    n = T // L

    def to_chunks(a):
        return a.reshape(B, n, L, H, a.shape[-1]).transpose(1, 0, 3, 2, 4)

    causal = jnp.tril(jnp.ones((L, L), dtype=bool))

    def step(S, inp):
        q_, k_, v_, g_ = inp
        b = jnp.cumsum(g_, axis=2)
        o_inter = jnp.einsum('bhld,bhde->bhle', q_ * jnp.exp(b), S)
        diff = jnp.where(causal[None, None, :, :, None],
                         b[:, :, :, None, :] - b[:, :, None, :, :], -jnp.inf)
        attn = jnp.einsum('bhtd,bhsd,bhtsd->bhts', q_, k_, jnp.exp(diff))
        o_intra = jnp.einsum('bhts,bhse->bhte', attn, v_)
        b_last = b[:, :, -1:, :]
        S_new = jnp.exp(b_last[:, :, 0, :])[..., None] * S + jnp.einsum(
            'bhld,bhle->bhde', k_ * jnp.exp(b_last - b), v_)
        return S_new, o_inter + o_intra

    S0 = jnp.zeros((B, H, dk, dv), jnp.float32)
    _, o = lax.scan(step, S0, (to_chunks(q), to_chunks(k), to_chunks(v), to_chunks(log_a)))
    return o.transpose(1, 0, 3, 2, 4).reshape(B, T, H, dv)


def gla_mixer(q, k, v, og, lr_f, lr_b, wa_f, ba_f, wa_b, ba_b, norm_g):
    B, T, _ = q.shape
    f32 = jnp.float32

    def heads(a, d):
        return a.astype(f32).reshape(B, T, GLA_HEADS, d)

    qh = heads(q, GLA_DK) * (GLA_DK ** -0.5)
    kh = heads(k, GLA_DK)
    vh = heads(v, GLA_DV)

    def log_decay(lr, wa, ba):
        return heads(jax.nn.log_sigmoid((lr @ wa + ba).astype(f32)) / GLA_TAU, GLA_DK)

    def rev(a):
        return jnp.flip(a, axis=1)

    o = gla_chunked(qh, kh, vh, log_decay(lr_f, wa_f, ba_f)) + rev(
        gla_chunked(rev(qh), rev(kh), rev(vh), rev(log_decay(lr_b, wa_b, ba_b))))
    o = rms_norm(o, norm_g).reshape(B, T, GLA_HEADS * GLA_DV)
    return (o * jax.nn.silu(og.astype(f32))).astype(q.dtype)


def short_conv(bg, cg, hv, w):
    u = cg * hv
    y = lax.conv_general_dilated(u, w[:, None, :].astype(u.dtype), window_strides=(1,),
                                 padding=((CONV_WIDTH // 2, CONV_WIDTH // 2),),
                                 dimension_numbers=('NWC', 'WIO', 'NWC'),
                                 feature_group_count=CONV_CH)
    return bg * y


def windowed_gqa(q, k, v, sink):
    B, T, Hq, d = q.shape
    Hkv = k.shape[2]
    G = Hq // Hkv
    W = ATT_BLOCK
    NB = T // W
    f32 = jnp.float32
    qb = q.astype(f32).reshape(B, NB, W, Hkv, G, d)

    def neighbours(a):
        ap = jnp.pad(a.astype(f32), ((0, 0), (W, W), (0, 0), (0, 0))).reshape(B, NB + 2, W, Hkv, d)
        return jnp.concatenate([ap[:, :-2], ap[:, 1:-1], ap[:, 2:]], axis=2)

    kb = neighbours(k)
    vb = neighbours(v)
    s = jnp.einsum('bnqhgd,bnkhd->bnhgqk', qb, kb) * (d ** -0.5)
    rel = jnp.arange(3 * W)[None, :] - W - jnp.arange(W)[:, None]
    kpos = (jnp.arange(NB) * W - W)[:, None, None] + jnp.arange(3 * W)[None, None, :]
    valid = (jnp.abs(rel) <= ATT_WINDOW)[None] & (kpos >= 0) & (kpos < T)
    slopes = jnp.asarray(2.0 ** (-8.0 * (np.arange(Hq) + 1) / Hq), f32).reshape(Hkv, G)
    bias = -slopes[:, :, None, None] * jnp.abs(rel).astype(f32)[None, None]
    s = jnp.where(valid[None, :, None, None], s + bias[None, None], -jnp.inf)
    sk = sink.astype(f32).reshape(Hkv, G)[None, None, :, :, None, None]
    m = jnp.maximum(jnp.max(s, axis=-1, keepdims=True), sk)
    p = jnp.exp(s - m)
    p = p / (jnp.sum(p, axis=-1, keepdims=True) + jnp.exp(sk - m))
    o = jnp.einsum('bnhgqk,bnkhd->bnqhgd', p, vb)
    return o.reshape(B, T, Hq * d)


def complex_affine_combine(e1, e2):
    ar1, ai1, br1, bi1 = e1
    ar2, ai2, br2, bi2 = e2
    return (ar1 * ar2 - ai1 * ai2,
            ar1 * ai2 + ai1 * ar2,
            ar2 * br1 - ai2 * bi1 + br2,
            ar2 * bi1 + ai2 * br1 + bi2)


def s5_mixer(u, lam_re, lam_im, log_step, b_re, b_im, c_re, c_im, d_skip, glu_w, glu_b):
    B, T, _ = u.shape
    f32 = jnp.float32
    uf = u.astype(f32)
    ug = uf.reshape(B, T, S5_GROUPS, S5_GROUP)
    y = d_skip.astype(f32) * uf
    for direction in range(2):
        lr = lam_re[direction].astype(f32)
        li = lam_im[direction].astype(f32)
        dt = jnp.exp(log_step[direction].astype(f32))[:, None]
        mag = jnp.exp(lr * dt)
        ar = mag * jnp.cos(li * dt)
        ai = mag * jnp.sin(li * dt)
        den = lr * lr + li * li
        zr = ((ar - 1.0) * lr + ai * li) / den
        zi = (ai * lr - (ar - 1.0) * li) / den
        br_ = b_re[direction].astype(f32)
        bi_ = b_im[direction].astype(f32)
        bbar_r = zr[..., None] * br_ - zi[..., None] * bi_
        bbar_i = zr[..., None] * bi_ + zi[..., None] * br_
        bu_r = jnp.einsum('btgc,gpc->btgp', ug, bbar_r)
        bu_i = jnp.einsum('btgc,gpc->btgp', ug, bbar_i)
        shape = (1, T, S5_GROUPS, S5_STATE)
        _, _, sr, si = lax.associative_scan(
            complex_affine_combine,
            (jnp.broadcast_to(ar, shape), jnp.broadcast_to(ai, shape), bu_r, bu_i),
            reverse=(direction == 1), axis=1)
        yd = (jnp.einsum('btgp,gcp->btgc', sr, c_re[direction].astype(f32))
              - jnp.einsum('btgp,gcp->btgc', si, c_im[direction].astype(f32)))
        y = y + yd.reshape(B, T, S5_CH)
    z = jax.nn.gelu(y)
    gate = jax.nn.sigmoid(z @ glu_w.astype(f32) + glu_b.astype(f32))
    return (z * gate).astype(u.dtype)


def moe_ffn(h, router_w, router_b, w_gu, b_gu, w_down, b_down):
    Bq, T, D = h.shape
    N = Bq * T
    NK = N * TOP_K
    xt = h.reshape(N, D)
    logits = (xt @ router_w + router_b).astype(jnp.float32)
    top_logit, top_idx = lax.top_k(logits, TOP_K)
    gate = jax.nn.softmax(top_logit, axis=-1)
    flat_e = top_idx.reshape(NK).astype(jnp.int32)
    flat_tok = jnp.arange(NK, dtype=jnp.int32) // TOP_K
    order = jnp.argsort(flat_e)
    sorted_e = flat_e[order]
    sorted_tok = flat_tok[order]
    sorted_gate = gate.reshape(NK)[order]
    counts = jnp.zeros((N_EXPERTS,), jnp.int32).at[flat_e].add(1)
    group_start = jnp.cumsum(counts) - counts
    padded = (counts + MOE_BLOCK - 1) // MOE_BLOCK * MOE_BLOCK
    padded_end = jnp.cumsum(padded)
    padded_start = padded_end - padded
    dest = padded_start[sorted_e] + jnp.arange(NK, dtype=jnp.int32) - group_start[sorted_e]
    n_blocks = -(-(NK + N_EXPERTS * (MOE_BLOCK - 1)) // MOE_BLOCK)
    n_rows = n_blocks * MOE_BLOCK
    row_tok = jnp.full((n_rows,), N, jnp.int32).at[dest].set(sorted_tok)
    x_pad = jnp.concatenate([xt, jnp.zeros((1, D), xt.dtype)], axis=0)
    x_rows = x_pad[row_tok].reshape(n_blocks, MOE_BLOCK, D)
    block_e = jnp.minimum(jnp.searchsorted(padded_end, jnp.arange(n_blocks, dtype=jnp.int32) * MOE_BLOCK,
                                           side='right'), N_EXPERTS - 1)

    def expert_block(args):
        xb, e = args
        gu = xb @ w_gu[e] + b_gu[e]
        g_, u_ = jnp.split(gu, 2, axis=-1)
        g_ = jnp.minimum(g_, SWIGLU_LIMIT)
        u_ = jnp.clip(u_, -SWIGLU_LIMIT, SWIGLU_LIMIT)
        hh = g_ * jax.nn.sigmoid(SWIGLU_ALPHA * g_) * (u_ + 1.0)
        return hh @ w_down[e] + b_down[e]

    y_rows = lax.map(expert_block, (x_rows, block_e)).reshape(n_rows, D)
    contrib = y_rows[dest] * sorted_gate[:, None].astype(h.dtype)
    y = jnp.zeros((N, D), h.dtype).at[sorted_tok].add(contrib)
    return y.reshape(Bq, T, D)


def even_mixer(h, p, i):
    q, k, v, og, lr_f, lr_b, bg, cg, hv = split_cols(h @ p['ev_w_in'][i], EVEN_SPLITS)
    a_out = gla_mixer(q, k, v, og, lr_f, lr_b, p['gla_wa_f'][i], p['gla_ba_f'][i],
                      p['gla_wa_b'][i], p['gla_ba_b'][i], p['gla_norm_g'][i])
    b_out = short_conv(bg, cg, hv, p['conv_w'][i])
    return jnp.concatenate([a_out, b_out], axis=-1) @ p['ev_w_out'][i]


def odd_mixer(h, p, i):
    B, T, _ = h.shape
    cq, ck, cv, du = split_cols(h @ p['od_w_in'][i], ODD_SPLITS)
    q = rms_norm(cq.reshape(B, T, ATT_Q_HEADS, HEAD_DIM), p['q_norm_g'][i])
    k = rms_norm(ck.reshape(B, T, ATT_KV_HEADS, HEAD_DIM), p['k_norm_g'][i])
    v = cv.reshape(B, T, ATT_KV_HEADS, HEAD_DIM)
    c_out = windowed_gqa(q, k, v, p['attn_sink'][i]).astype(h.dtype)
    d_out = s5_mixer(du, p['s5_lam_re'][i], p['s5_lam_im'][i], p['s5_log_step'][i],
                     p['s5_b_re'][i], p['s5_b_im'][i], p['s5_c_re'][i], p['s5_c_im'][i],
                     p['s5_d'][i], p['s5_glu_w'][i], p['s5_glu_b'][i])
    return jnp.concatenate([c_out, d_out], axis=-1) @ p['od_w_out'][i]


def trunk(x, c, p):
    Bq = c.shape[0]
    for layer in range(DEPTH):
        mod = (jax.nn.silu(c) @ p['ada_w'][layer] + p['ada_b'][layer]).reshape(Bq, 6, 1, D_MODEL)
        sh1, sc1, g1, sh2, sc2, g2 = [mod[:, j] for j in range(6)]
        h = rms_norm(x, p['norm1_g'][layer]) * (1.0 + sc1) + sh1
        if layer % 2 == 0:
            mix = even_mixer(h, p, layer // 2)
        else:
            mix = odd_mixer(h, p, layer // 2)
        x = x + g1 * mix
        h = rms_norm(x, p['norm2_g'][layer]) * (1.0 + sc2) + sh2
        x = x + g2 * moe_ffn(h, p['router_w'][layer], p['router_b'][layer], p['exp_w_gu'][layer],
                             p['exp_b_gu'][layer], p['exp_w_down'][layer], p['exp_b_down'][layer])
    return x


def setup_inputs(seed: int = 0) -> dict:
    key = jax.random.key(seed)
    ks = iter(jax.random.split(key, 48))

    def nrm(shape, std):
        return std * jax.random.normal(next(ks), shape, jnp.float32)

    n_idx = jnp.arange(S5_STATE, dtype=jnp.float32)
    s5_shape = (N_ODD, 2, S5_GROUPS, S5_STATE)
    return {
        'x_prompt': nrm((BATCH, SEQ, D_MODEL), 1.0),
        'x_sample': nrm((DEC_BATCH, DEC_SEQ, D_MODEL), 1.0),
        'c_prompt': nrm((BATCH, D_MODEL), 1.0),
        'c_sample': nrm((DEC_BATCH, D_MODEL), 1.0),
        'ada_w': nrm((DEPTH, D_MODEL, 6 * D_MODEL), 0.5 * D_MODEL ** -0.5),
        'ada_b': nrm((DEPTH, 6 * D_MODEL), 0.02),
        'norm1_g': 1.0 + nrm((DEPTH, D_MODEL), 0.02),
        'norm2_g': 1.0 + nrm((DEPTH, D_MODEL), 0.02),
        'ev_w_in': nrm((N_EVEN, D_MODEL, EVEN_IN), D_MODEL ** -0.5),
        'ev_w_out': nrm((N_EVEN, MIX_WIDTH, D_MODEL), MIX_WIDTH ** -0.5),
        'gla_wa_f': nrm((N_EVEN, GLA_RANK, GLA_HEADS * GLA_DK), GLA_RANK ** -0.5),
        'gla_ba_f': nrm((N_EVEN, GLA_HEADS * GLA_DK), 0.02),
        'gla_wa_b': nrm((N_EVEN, GLA_RANK, GLA_HEADS * GLA_DK), GLA_RANK ** -0.5),
        'gla_ba_b': nrm((N_EVEN, GLA_HEADS * GLA_DK), 0.02),
        'gla_norm_g': 1.0 + nrm((N_EVEN, GLA_DV), 0.02),
        'conv_w': nrm((N_EVEN, CONV_WIDTH, CONV_CH), CONV_WIDTH ** -0.5),
        'od_w_in': nrm((N_ODD, D_MODEL, ODD_IN), D_MODEL ** -0.5),
        'od_w_out': nrm((N_ODD, MIX_WIDTH, D_MODEL), MIX_WIDTH ** -0.5),
        'q_norm_g': 1.0 + nrm((N_ODD, HEAD_DIM), 0.02),
        'k_norm_g': 1.0 + nrm((N_ODD, HEAD_DIM), 0.02),
        'attn_sink': nrm((N_ODD, ATT_Q_HEADS), 0.5),
        's5_lam_re': -0.5 + nrm(s5_shape, 0.01),
        's5_lam_im': math.pi * n_idx + nrm(s5_shape, 0.01),
        's5_log_step': jax.random.uniform(next(ks), (N_ODD, 2, S5_GROUPS), jnp.float32,
                                          math.log(1e-3), math.log(1e-1)),
        's5_b_re': nrm((N_ODD, 2, S5_GROUPS, S5_STATE, S5_GROUP), (2 * S5_GROUP) ** -0.5),
        's5_b_im': nrm((N_ODD, 2, S5_GROUPS, S5_STATE, S5_GROUP), (2 * S5_GROUP) ** -0.5),
        's5_c_re': nrm((N_ODD, 2, S5_GROUPS, S5_GROUP, S5_STATE), (2 * S5_STATE) ** -0.5),
        's5_c_im': nrm((N_ODD, 2, S5_GROUPS, S5_GROUP, S5_STATE), (2 * S5_STATE) ** -0.5),
        's5_d': nrm((N_ODD, S5_CH), 1.0),
        's5_glu_w': nrm((N_ODD, S5_CH, S5_CH), S5_CH ** -0.5),
        's5_glu_b': nrm((N_ODD, S5_CH), 0.02),
        'router_w': nrm((DEPTH, D_MODEL, N_EXPERTS), D_MODEL ** -0.5),
        'router_b': nrm((DEPTH, N_EXPERTS), 0.01),
        'exp_w_gu': nrm((DEPTH, N_EXPERTS, D_MODEL, 2 * D_FF), D_MODEL ** -0.5),
        'exp_b_gu': nrm((DEPTH, N_EXPERTS, 2 * D_FF), 0.02),
        'exp_w_down': nrm((DEPTH, N_EXPERTS, D_FF, D_MODEL), D_FF ** -0.5),
        'exp_b_down': nrm((DEPTH, N_EXPERTS, D_MODEL), 0.02),
    }


def reference(x_prompt, x_sample, c_prompt, c_sample, ada_w, ada_b, norm1_g, norm2_g,
              ev_w_in, ev_w_out, gla_wa_f, gla_ba_f, gla_wa_b, gla_ba_b, gla_norm_g, conv_w,
              od_w_in, od_w_out, q_norm_g, k_norm_g, attn_sink, s5_lam_re, s5_lam_im, s5_log_step,
              s5_b_re, s5_b_im, s5_c_re, s5_c_im, s5_d, s5_glu_w, s5_glu_b,
              router_w, router_b, exp_w_gu, exp_b_gu, exp_w_down, exp_b_down):
    p = dict(ada_w=ada_w, ada_b=ada_b, norm1_g=norm1_g, norm2_g=norm2_g,
             ev_w_in=ev_w_in, ev_w_out=ev_w_out, gla_wa_f=gla_wa_f, gla_ba_f=gla_ba_f,
             gla_wa_b=gla_wa_b, gla_ba_b=gla_ba_b, gla_norm_g=gla_norm_g, conv_w=conv_w,
             od_w_in=od_w_in, od_w_out=od_w_out, q_norm_g=q_norm_g, k_norm_g=k_norm_g,
             attn_sink=attn_sink, s5_lam_re=s5_lam_re, s5_lam_im=s5_lam_im, s5_log_step=s5_log_step,
             s5_b_re=s5_b_re, s5_b_im=s5_b_im, s5_c_re=s5_c_re, s5_c_im=s5_c_im, s5_d=s5_d,
             s5_glu_w=s5_glu_w, s5_glu_b=s5_glu_b, router_w=router_w, router_b=router_b,
             exp_w_gu=exp_w_gu, exp_b_gu=exp_b_gu, exp_w_down=exp_w_down, exp_b_down=exp_b_down)
    y_prompt = trunk(x_prompt, c_prompt, p)
    y_sample = trunk(x_sample, c_sample, p)
    return (y_prompt, y_sample)
```

```python
import functools
import math
from typing import NamedTuple

import numpy as np
import jax
import jax.numpy as jnp
from jax import lax
from jax.experimental import pallas as pl
from jax.experimental.pallas import tpu as pltpu

F32 = jnp.float32
BF16 = jnp.bfloat16

D_MODEL = 1024
DEPTH = 4
HALF = 512
HEAD_DIM = 64

GLA_HEADS = 4
GLA_DV = 128
GLA_DK = 64
GLA_RANK = 16
GLA_TAU = 16.0
GLA_CHUNK = 64

CONV_WIDTH = 3

ATT_Q_HEADS = 8
ATT_KV_HEADS = 2
ATT_WINDOW = 128

S5_GROUP = 16
S5_GROUPS = 32
S5_STATE = 64
S5_CHUNK = 16

N_EXPERTS = 32
TOP_K = 4
SWIGLU_LIMIT = 7.0
SWIGLU_ALPHA = 1.702
MOE_BLOCK = 256

NORM_EPS = 1e-6
NEG_BIG = -1e30

VMEM_LIMIT_BYTES = 52 * 1024 * 1024

NT_DIMS = (((1,), (1,)), ((), ()))
TN_DIMS = (((0,), (0,)), ((), ()))


class Segs(NamedTuple):
    seg: int
    nseg: int
    join: int


def _params(n_axes=1):
    return pltpu.CompilerParams(dimension_semantics=("arbitrary",) * n_axes,
                                vmem_limit_bytes=VMEM_LIMIT_BYTES)


def _split_bf16(a):
    hi = a.astype(BF16)
    lo = (a - hi.astype(F32)).astype(BF16)
    return hi, lo


def _dot(a, b):
    return jnp.dot(a, b, preferred_element_type=F32)


def _dot_split(a, b):
    a_hi, a_lo = _split_bf16(a)
    b_hi, b_lo = _split_bf16(b)
    return _dot(a_hi, b_hi) + _dot(a_lo, b_hi) + _dot(a_hi, b_lo)


def _rms_mod(x, g, sc, sh):
    ms = jnp.mean(x * x, axis=-1, keepdims=True)
    h = x * lax.rsqrt(ms + NORM_EPS) * g
    return h * (1.0 + sc) + sh


def _ada_kernel(c_ref, w_ref, b_ref, o_ref):
    c = c_ref[...]
    s = c * jax.nn.sigmoid(c)
    o_ref[0] = _dot_split(s, w_ref[0]) + b_ref[0]


def ada_modulation(c_pad, ada_w, ada_b):
    rows = c_pad.shape[0]
    cols = 6 * D_MODEL
    tn = 1536
    return pl.pallas_call(
        _ada_kernel,
        grid=(DEPTH, cols // tn),
        in_specs=[pl.BlockSpec((rows, D_MODEL), lambda l, j: (0, 0)),
                  pl.BlockSpec((1, D_MODEL, tn), lambda l, j: (l, 0, j)),
                  pl.BlockSpec((1, 1, tn), lambda l, j: (l, 0, j))],
        out_specs=pl.BlockSpec((1, rows, tn), lambda l, j: (l, 0, j)),
        out_shape=jax.ShapeDtypeStruct((DEPTH, rows, cols), F32),
        compiler_params=_params(2),
        name="ada_modulation",
    )(c_pad, ada_w, ada_b.reshape(DEPTH, 1, cols))


def _norm_proj_kernel(x_ref, g_ref, sc_ref, sh_ref, *refs, n_w):
    h = _rms_mod(x_ref[...], g_ref[...], sc_ref[0], sh_ref[0]).astype(BF16)
    for w_ref, o_ref in zip(refs[:n_w], refs[n_w:]):
        o_ref[...] = _dot(h, w_ref[...]).astype(o_ref.dtype)


def norm_proj(x, g, sc, sh, weights, segs, tm=512):
    n = x.shape[0]
    n_w = len(weights)
    seg_of = lambda i: (i * tm) // segs.seg
    in_specs = [pl.BlockSpec((tm, D_MODEL), lambda i: (i, 0)),
                pl.BlockSpec((1, D_MODEL), lambda i: (0, 0)),
                pl.BlockSpec((1, 1, D_MODEL), lambda i: (seg_of(i), 0, 0)),
                pl.BlockSpec((1, 1, D_MODEL), lambda i: (seg_of(i), 0, 0))]
    in_specs += [pl.BlockSpec(w.shape, lambda i: (0, 0)) for w in weights]
    return pl.pallas_call(
        functools.partial(_norm_proj_kernel, n_w=n_w),
        grid=(n // tm,),
        in_specs=in_specs,
        out_specs=[pl.BlockSpec((tm, w.shape[1]), lambda i: (i, 0)) for w in weights],
        out_shape=[jax.ShapeDtypeStruct((n, w.shape[1]), F32) for w in weights],
        compiler_params=_params(),
        name="norm_proj",
    )(x, g, sc, sh, *weights)


def _gla_kernel(qk_f, v_f, lr_f, qk_b, v_b, lr_b, wa_f, ba_f, wa_b, ba_b,
                of_ref, ob_ref, st_f, st_b, *, segs, rows):
    i = pl.program_id(0)
    nb = pl.num_programs(0)
    ib = nb - 1 - i
    bps = segs.seg // rows
    seg_f = i // bps
    seg_b = ib // bps

    @pl.when((i % bps == 0) & (seg_f != segs.join))
    def _():
        st_f[...] = jnp.zeros_like(st_f)

    @pl.when((ib % bps == bps - 1) & (seg_b + 1 != segs.join))
    def _():
        st_b[...] = jnp.zeros_like(st_b)

    L = GLA_CHUNK
    r_io = lax.broadcasted_iota(jnp.int32, (L, L), 0)
    c_io = lax.broadcasted_iota(jnp.int32, (L, L), 1)
    lower = r_io >= c_io
    upper = r_io <= c_io
    scale = GLA_DK ** -0.5
    n_chunks = rows // L

    def one_chunk(c, qk_ref, v_ref, lr_ref, wa_ref, ba_ref, st_ref, o_ref, mask, edge_row):
        r0 = pl.multiple_of(c * L, L)
        qk = qk_ref[pl.ds(r0, L), :]
        q = qk[:, :GLA_HEADS * GLA_DK] * scale
        k = qk[:, GLA_HEADS * GLA_DK:]
        v = v_ref[pl.ds(r0, L), :]
        pre = _dot_split(lr_ref[pl.ds(r0, L), :], wa_ref[...]) + ba_ref[...]
        log_sig = jnp.minimum(pre, 0.0) - jnp.log(1.0 + jnp.exp(-jnp.abs(pre)))
        g = log_sig * (1.0 / GLA_TAU)
        g_hi, g_lo = _split_bf16(g)
        tri = mask.astype(BF16)
        b = _dot(tri, g_hi) + _dot(tri, g_lo)
        b_edge = b[edge_row:edge_row + 1, :]
        qe = q * jnp.exp(b)
        ke = k * jnp.exp(-b)
        kd = k * jnp.exp(b_edge - b)
        d_edge = jnp.exp(b_edge)
        outs = []
        for h in range(GLA_HEADS):
            ks = slice(h * GLA_DK, (h + 1) * GLA_DK)
            vs = slice(h * GLA_DV, (h + 1) * GLA_DV)
            qeh = qe[:, ks].astype(BF16)
            keh = ke[:, ks].astype(BF16)
            kdh = kd[:, ks].astype(BF16)
            vh = v[:, vs].astype(BF16)
            st = st_ref[h]
            att = lax.dot_general(qeh, keh, NT_DIMS, preferred_element_type=F32)
            att = jnp.where(mask, att, 0.0).astype(BF16)
            o = lax.dot_general(qeh, st.astype(BF16), NT_DIMS, preferred_element_type=F32)
            o = o + _dot(att, vh)
            st_ref[h] = st * d_edge[:, ks] + lax.dot_general(vh, kdh, TN_DIMS, preferred_element_type=F32)
            outs.append(o)
        o_ref[pl.ds(r0, L), :] = jnp.concatenate(outs, axis=1)

    def body(c, carry):
        one_chunk(c, qk_f, v_f, lr_f, wa_f, ba_f, st_f, of_ref, lower, L - 1)
        one_chunk(n_chunks - 1 - c, qk_b, v_b, lr_b, wa_b, ba_b, st_b, ob_ref, upper, 0)
        return carry

    lax.fori_loop(0, n_chunks, body, 0)


def gla_bidir(a_proj, lr_proj, wa_f, ba_f, wa_b, ba_b, segs, rows=512):
    n = a_proj.shape[0]
    nb = n // rows
    lanes = lr_proj.shape[1]
    wa_f_pad = jnp.zeros((lanes, GLA_HEADS * GLA_DK), F32).at[:GLA_RANK].set(wa_f)
    wa_b_pad = jnp.zeros((lanes, GLA_HEADS * GLA_DK), F32).at[GLA_RANK:2 * GLA_RANK].set(wa_b)
    fwd = lambda i: (i, 0)
    bwd = lambda i: (nb - 1 - i, 0)
    const = lambda i: (0, 0)
    spec = lambda width, col, row_map: pl.BlockSpec((rows, width), lambda i: (row_map(i)[0], col))
    return pl.pallas_call(
        functools.partial(_gla_kernel, segs=segs, rows=rows),
        grid=(nb,),
        in_specs=[spec(512, 0, fwd), spec(512, 1, fwd), pl.BlockSpec((rows, lanes), fwd),
                  spec(512, 0, bwd), spec(512, 1, bwd), pl.BlockSpec((rows, lanes), bwd),
                  pl.BlockSpec(wa_f_pad.shape, const), pl.BlockSpec((1, 256), const),
                  pl.BlockSpec(wa_b_pad.shape, const), pl.BlockSpec((1, 256), const)],
        out_specs=[pl.BlockSpec((rows, HALF), fwd), pl.BlockSpec((rows, HALF), bwd)],
        out_shape=[jax.ShapeDtypeStruct((n, HALF), F32)] * 2,
        scratch_shapes=[pltpu.VMEM((GLA_HEADS, GLA_DV, GLA_DK), F32)] * 2,
        compiler_params=_params(),
        name="gla_bidir",
    )(a_proj, a_proj, lr_proj, a_proj, a_proj, lr_proj,
      wa_f_pad, ba_f.reshape(1, -1), wa_b_pad, ba_b.reshape(1, -1))


def _even_out_kernel(x_ref, of_ref, ob_ref, og_ref, c_ref, cprev_ref, cnext_ref, ng_ref, cw_ref,
                     w_ref, g1_ref, o_ref, *, segs, tm):
    i = pl.program_id(0)
    tps = segs.seg // tm
    seg = i // tps
    has_prev = jnp.logical_not((i % tps == 0) & (seg != segs.join))
    has_next = jnp.logical_not((i % tps == tps - 1) & (seg + 1 != segs.join))

    o = of_ref[...] + ob_ref[...]
    og = og_ref[...]
    parts = []
    for h in range(GLA_HEADS):
        oh = o[:, h * GLA_DV:(h + 1) * GLA_DV]
        ms = jnp.mean(oh * oh, axis=-1, keepdims=True)
        parts.append(oh * lax.rsqrt(ms + NORM_EPS) * ng_ref[...])
    a_out = jnp.concatenate(parts, axis=1) * (og * jax.nn.sigmoid(og))

    c = c_ref[...]
    bg = c[:, :HALF]
    u = c[:, HALF:2 * HALF] * c[:, 2 * HALF:]
    cp = cprev_ref[7:8, :]
    cn = cnext_ref[0:1, :]
    u_prev_edge = jnp.where(has_prev, cp[:, HALF:2 * HALF] * cp[:, 2 * HALF:], 0.0)
    u_next_edge = jnp.where(has_next, cn[:, HALF:2 * HALF] * cn[:, 2 * HALF:], 0.0)
    row = lax.broadcasted_iota(jnp.int32, (tm, 1), 0)
    u_prev = jnp.where(row == 0, u_prev_edge, pltpu.roll(u, 1, axis=0))
    u_next = jnp.where(row == tm - 1, u_next_edge, pltpu.roll(u, tm - 1, axis=0))
    cw = cw_ref[...]
    b_out = bg * (cw[0:1, :] * u_prev + cw[1:2, :] * u + cw[2:3, :] * u_next)

    w = w_ref[...]
    mix = _dot(a_out.astype(BF16), w[:HALF, :]) + _dot(b_out.astype(BF16), w[HALF:, :])
    o_ref[...] = x_ref[...] + g1_ref[0] * mix


def even_out(x, o_f, o_b, a_proj, c_proj, norm_g, conv_w, w_out, g1, segs, tm=512):
    n = x.shape[0]
    last8 = n // 8 - 1
    seg_of = lambda i: (i * tm) // segs.seg
    return pl.pallas_call(
        functools.partial(_even_out_kernel, segs=segs, tm=tm),
        grid=(n // tm,),
        in_specs=[pl.BlockSpec((tm, D_MODEL), lambda i: (i, 0)),
                  pl.BlockSpec((tm, HALF), lambda i: (i, 0)),
                  pl.BlockSpec((tm, HALF), lambda i: (i, 0)),
                  pl.BlockSpec((tm, HALF), lambda i: (i, 2)),
                  pl.BlockSpec((tm, 3 * HALF), lambda i: (i, 0)),
                  pl.BlockSpec((8, 3 * HALF), lambda i: (jnp.maximum(i * (tm // 8) - 1, 0), 0)),
                  pl.BlockSpec((8, 3 * HALF), lambda i: (jnp.minimum((i + 1) * (tm // 8), last8), 0)),
                  pl.BlockSpec((1, GLA_DV), lambda i: (0, 0)),
                  pl.BlockSpec((CONV_WIDTH, HALF), lambda i: (0, 0)),
                  pl.BlockSpec((D_MODEL, D_MODEL), lambda i: (0, 0)),
                  pl.BlockSpec((1, 1, D_MODEL), lambda i: (seg_of(i), 0, 0))],
        out_specs=pl.BlockSpec((tm, D_MODEL), lambda i: (i, 0)),
        out_shape=jax.ShapeDtypeStruct((n, D_MODEL), F32),
        compiler_params=_params(),
        name="even_out",
    )(x, o_f, o_b, a_proj, c_proj, c_proj, c_proj, norm_g.reshape(1, -1), conv_w, w_out, g1)


def _attn_kernel(q_ref, kv_ref, kvp_ref, kvn_ref, qg_ref, kg_ref, sink_ref, o_ref, *, segs, tq):
    i = pl.program_id(0)
    W = ATT_WINDOW
    tps = segs.seg // tq
    seg = i // tps
    first = (i % tps == 0) & (seg != segs.join)
    last = (i % tps == tps - 1) & (seg + 1 != segs.join)

    kv_all = jnp.concatenate([kvp_ref[...], kv_ref[...], kvn_ref[...]], axis=0)
    kvw = ATT_KV_HEADS * HEAD_DIM
    k_heads, v_heads = [], []
    for h in range(ATT_KV_HEADS):
        kh = kv_all[:, h * HEAD_DIM:(h + 1) * HEAD_DIM]
        ms = jnp.mean(kh * kh, axis=-1, keepdims=True)
        k_heads.append((kh * lax.rsqrt(ms + NORM_EPS) * kg_ref[...]).astype(BF16))
        v_heads.append(kv_all[:, kvw + h * HEAD_DIM:kvw + (h + 1) * HEAD_DIM].astype(BF16))

    t_io = lax.broadcasted_iota(jnp.int32, (W, 3 * W), 0)
    j_io = lax.broadcasted_iota(jnp.int32, (W, 3 * W), 1)
    rel = j_io - W - t_io
    dist = jnp.abs(rel)
    in_window = dist <= W
    dist_f = dist.astype(F32)
    group = ATT_Q_HEADS // ATT_KV_HEADS
    n_blk = tq // W
    for blk in range(n_blk):
        valid = in_window
        if blk == 0:
            valid = valid & ((j_io >= W) | jnp.logical_not(first))
        if blk == n_blk - 1:
            valid = valid & ((j_io < 2 * W) | jnp.logical_not(last))
        q = q_ref[blk * W:(blk + 1) * W, :]
        outs = []
        for hq in range(ATT_Q_HEADS):
            kvh = hq // group
            qh = q[:, hq * HEAD_DIM:(hq + 1) * HEAD_DIM]
            ms = jnp.mean(qh * qh, axis=-1, keepdims=True)
            qn = (qh * lax.rsqrt(ms + NORM_EPS) * qg_ref[...] * (HEAD_DIM ** -0.5)).astype(BF16)
            kh = k_heads[kvh][blk * W:(blk + 3) * W, :]
            vh = v_heads[kvh][blk * W:(blk + 3) * W, :]
            s = lax.dot_general(qn, kh, NT_DIMS, preferred_element_type=F32)
            slope = 2.0 ** (-8.0 * (hq + 1) / ATT_Q_HEADS)
            s = jnp.where(valid, s - slope * dist_f, NEG_BIG)
            sk = sink_ref[hq]
            m = jnp.maximum(jnp.max(s, axis=-1, keepdims=True), sk)
            p = jnp.exp(s - m)
            denom = jnp.sum(p, axis=-1, keepdims=True) + jnp.exp(sk - m)
            outs.append(_dot(p.astype(BF16), vh) / denom)
        o_ref[blk * W:(blk + 1) * W, :] = jnp.concatenate(outs, axis=1)


def windowed_attention(q_proj, kv_proj, q_norm_g, k_norm_g, sink, segs, tq=512):
    n = q_proj.shape[0]
    W = ATT_WINDOW
    r = tq // W
    last = n // W - 1
    kvw = 2 * ATT_KV_HEADS * HEAD_DIM
    return pl.pallas_call(
        functools.partial(_attn_kernel, segs=segs, tq=tq),
        grid=(n // tq,),
        in_specs=[pl.BlockSpec((tq, HALF), lambda i: (i, 0)),
                  pl.BlockSpec((tq, kvw), lambda i: (i, 0)),
                  pl.BlockSpec((W, kvw), lambda i: (jnp.maximum(i * r - 1, 0), 0)),
                  pl.BlockSpec((W, kvw), lambda i: (jnp.minimum((i + 1) * r, last), 0)),
                  pl.BlockSpec((1, HEAD_DIM), lambda i: (0, 0)),
                  pl.BlockSpec((1, HEAD_DIM), lambda i: (0, 0)),
                  pl.BlockSpec(memory_space=pltpu.SMEM)],
        out_specs=pl.BlockSpec((tq, HALF), lambda i: (i, 0)),
        out_shape=jax.ShapeDtypeStruct((n, HALF), F32),
        compiler_params=_params(),
        name="windowed_attention",
    )(q_proj, kv_proj, kv_proj, kv_proj, q_norm_g.reshape(1, -1), k_norm_g.reshape(1, -1), sink)


def s5_tables(lam_re, lam_im, log_step, b_re, b_im, c_re, c_im):
    L = S5_CHUNK
    dt = jnp.exp(log_step)[:, :, None]
    lr, li = lam_re, lam_im
    mag = jnp.exp(lr * dt)
    ar, ai = mag * jnp.cos(li * dt), mag * jnp.sin(li * dt)
    den = lr * lr + li * li
    zr = ((ar - 1.0) * lr + ai * li) / den
    zi = (ai * lr - (ar - 1.0) * li) / den
    bbr = zr[..., None] * b_re - zi[..., None] * b_im
    bbi = zr[..., None] * b_im + zi[..., None] * b_re
    tau = jnp.arange(L + 1, dtype=F32)[:, None, None, None]
    pmag = jnp.exp(lr[None] * dt[None] * tau)
    pang = li[None] * dt[None] * tau
    pr, pi = pmag * jnp.cos(pang), pmag * jnp.sin(pang)

    hp = lax.Precision.HIGHEST
    car = c_re[None] * pr[:, :, :, None, :] - c_im[None] * pi[:, :, :, None, :]
    cai = c_re[None] * pi[:, :, :, None, :] + c_im[None] * pr[:, :, :, None, :]
    kern = (jnp.einsum('ldgop,dgpi->ldgoi', car, bbr, precision=hp)
            - jnp.einsum('ldgop,dgpi->ldgoi', cai, bbi, precision=hp))
    s_idx = jnp.arange(L)[:, None]
    t_idx = jnp.arange(L)[None, :]
    lag_f = jnp.clip(t_idx - s_idx, 0, L)
    lag_b = jnp.clip(s_idx - t_idx, 0, L)
    kf = jnp.where((t_idx >= s_idx)[:, :, None, None, None], kern[:, 0][lag_f], 0.0)
    kb = jnp.where((s_idx >= t_idx)[:, :, None, None, None], kern[:, 1][lag_b], 0.0)
    m = (kf + kb).transpose(2, 0, 4, 1, 3).reshape(S5_GROUPS, L * S5_GROUP, L * S5_GROUP)

    abr = pr[..., None] * bbr[None] - pi[..., None] * bbi[None]
    abi = pr[..., None] * bbi[None] + pi[..., None] * bbr[None]
    e_f = L - 1 - jnp.arange(L)
    e_b = jnp.arange(L)
    to_rows = lambda a: a.transpose(1, 0, 3, 2).reshape(S5_GROUPS, L * S5_GROUP, S5_STATE)
    wp = jnp.concatenate([to_rows(abr[e_f, 0]), to_rows(abi[e_f, 0]),
                          to_rows(abr[e_b, 1]), to_rows(abi[e_b, 1])], axis=-1)

    o_f = jnp.arange(L) + 1
    o_b = L - jnp.arange(L)
    to_cols = lambda a: a.transpose(1, 3, 0, 2).reshape(S5_GROUPS, S5_STATE, L * S5_GROUP)
    wc = jnp.concatenate([to_cols(car[o_f, 0]), -to_cols(cai[o_f, 0]),
                          to_cols(car[o_b, 1]), -to_cols(cai[o_b, 1])], axis=1)

    dr, di = pr[L], pi[L]
    dec = jnp.stack([jnp.concatenate([dr[0], dr[0]], -1), jnp.concatenate([-di[0], di[0]], -1),
                     jnp.concatenate([dr[1], dr[1]], -1), jnp.concatenate([-di[1], di[1]], -1)], axis=1)
    return m.astype(BF16), wp.astype(BF16), wc, dec


def _s5_kernel(x_ref, m_ref, wp_ref, wc_ref, dec_ref, y_ref, pf_scr, pb_scr, sf_scr, sb_scr, *, segs):
    P2 = 2 * S5_STATE
    nper = segs.seg // S5_CHUNK
    nseg = segs.nseg
    x = x_ref[0]
    y_ref[0] = _dot(x, m_ref[0])
    p = _dot(x, wp_ref[0])
    pf_scr[...] = p[:, :P2]
    pb_scr[...] = p[:, P2:]
    dec = dec_ref[0]
    da_f, db_f, da_b, db_b = dec[0:1], dec[1:2], dec[2:3], dec[3:4]

    def step(s, p_in, da, db):
        return s * da + pltpu.roll(s, S5_STATE, axis=1) * db + p_in

    def scan(rows_f, rows_b, s_f0, s_b0):
        def body(n, carry):
            s_f, s_b = carry
            rf = rows_f(n)
            rb = rows_b(n)
            sf_scr[rf, :] = s_f
            sb_scr[rb, :] = s_b
            return step(s_f, pf_scr[rf, :], da_f, db_f), step(s_b, pb_scr[rb, :], da_b, db_b)
        return lax.fori_loop(0, nper, body, (s_f0, s_b0))

    zeros = jnp.zeros((nseg, P2), F32)
    end_f, end_b = scan(lambda n: pl.ds(n, nseg, stride=nper),
                        lambda n: pl.ds(nper - 1 - n, nseg, stride=nper), zeros, zeros)
    if segs.join >= 0:
        j = segs.join
        scan(lambda n: pl.ds(j * nper + n, 1), lambda n: pl.ds(j * nper - 1 - n, 1),
             end_f[j - 1:j, :], end_b[j:j + 1, :])

    wc_hi, wc_lo = _split_bf16(wc_ref[0])
    acc = y_ref[0]
    for s_scr, rows in ((sf_scr, slice(0, P2)), (sb_scr, slice(P2, 2 * P2))):
        s_hi, s_lo = _split_bf16(s_scr[...])
        acc = acc + _dot(s_hi, wc_hi[rows]) + _dot(s_lo, wc_hi[rows]) + _dot(s_hi, wc_lo[rows])
    y_ref[0] = acc


def s5_conv(du, tables, segs):
    n = du.shape[0]
    L = S5_CHUNK
    rows = n // L
    width = L * S5_GROUP
    m, wp, wc, dec = tables
    xg = du.astype(BF16).reshape(rows, L, S5_GROUPS, S5_GROUP).transpose(2, 0, 1, 3).reshape(S5_GROUPS, rows, width)
    grp = lambda g: (g, 0, 0)
    yg = pl.pallas_call(
        functools.partial(_s5_kernel, segs=segs),
        grid=(S5_GROUPS,),
        in_specs=[pl.BlockSpec((1, rows, width), grp),
                  pl.BlockSpec((1, width, width), grp),
                  pl.BlockSpec((1, width, 4 * S5_STATE), grp),
                  pl.BlockSpec((1, 4 * S5_STATE, width), grp),
                  pl.BlockSpec((1, 4, 2 * S5_STATE), grp)],
        out_specs=pl.BlockSpec((1, rows, width), grp),
        out_shape=jax.ShapeDtypeStruct((S5_GROUPS, rows, width), F32),
        scratch_shapes=[pltpu.VMEM((rows, 2 * S5_STATE), F32)] * 4,
        compiler_params=_params(),
        name="s5_conv",
    )(xg, m, wp, wc, dec)
    return yg.reshape(S5_GROUPS, rows, L, S5_GROUP).transpose(1, 2, 0, 3).reshape(n, HALF)


def _odd_out_kernel(x_ref, att_ref, du_ref, ys_ref, dsk_ref, gw_ref, gb_ref, w_ref, g1_ref, o_ref):
    y = dsk_ref[...] * du_ref[...] + ys_ref[...]
    z = 0.5 * y * (1.0 + jnp.tanh(math.sqrt(2.0 / math.pi) * (y + 0.044715 * (y * y * y))))
    gate = jax.nn.sigmoid(_dot(z.astype(BF16), gw_ref[...]) + gb_ref[...])
    d_out = z * gate
    w = w_ref[...]
    mix = _dot(att_ref[...].astype(BF16), w[:HALF, :]) + _dot(d_out.astype(BF16), w[HALF:, :])
    o_ref[...] = x_ref[...] + g1_ref[0] * mix


def odd_out(x, att, du, ys, d_skip, glu_w, glu_b, w_out, g1, segs, tm=512):
    n = x.shape[0]
    seg_of = lambda i: (i * tm) // segs.seg
    row = lambda i: (i, 0)
    const = lambda i: (0, 0)
    return pl.pallas_call(
        _odd_out_kernel,
        grid=(n // tm,),
        in_specs=[pl.BlockSpec((tm, D_MODEL), row), pl.BlockSpec((tm, HALF), row),
                  pl.BlockSpec((tm, HALF), row), pl.BlockSpec((tm, HALF), row),
                  pl.BlockSpec((1, HALF), const), pl.BlockSpec((HALF, HALF), const),
                  pl.BlockSpec((1, HALF), const), pl.BlockSpec((D_MODEL, D_MODEL), const),
                  pl.BlockSpec((1, 1, D_MODEL), lambda i: (seg_of(i), 0, 0))],
        out_specs=pl.BlockSpec((tm, D_MODEL), row),
        out_shape=jax.ShapeDtypeStruct((n, D_MODEL), F32),
        compiler_params=_params(),
        name="odd_out",
    )(x, att, du, ys, d_skip.reshape(1, -1), glu_w, glu_b.reshape(1, -1), w_out, g1)


def _norm_router_kernel(x_ref, g_ref, sc_ref, sh_ref, wr_ref, br_ref, h_ref, idx_ref, gate_ref):
    h = _rms_mod(x_ref[...], g_ref[...], sc_ref[0], sh_ref[0])
    h_ref[...] = h
    h_hi, h_lo = _split_bf16(h)
    w_hi, w_lo = _split_bf16(wr_ref[...])
    nt = lambda a, b: lax.dot_general(a, b, NT_DIMS, preferred_element_type=F32)
    logits = nt(w_hi, h_hi) + nt(w_lo, h_hi) + nt(w_hi, h_lo) + br_ref[...]
    e_io = lax.broadcasted_iota(jnp.int32, logits.shape, 0)
    tops = []
    for k in range(TOP_K):
        m = jnp.max(logits, axis=0, keepdims=True)
        idx = jnp.min(jnp.where(logits == m, e_io, N_EXPERTS), axis=0, keepdims=True)
        idx_ref[k:k + 1, :] = idx
        logits = jnp.where(e_io == idx, -jnp.inf, logits)
        tops.append(m)
    es = [jnp.exp(t - tops[0]) for t in tops]
    total = es[0] + es[1] + es[2] + es[3]
    for k in range(TOP_K):
        gate_ref[k:k + 1, :] = es[k] / total


def norm_router(x, g, sc, sh, router_w, router_b, segs, tm=512):
    n = x.shape[0]
    seg_of = lambda i: (i * tm) // segs.seg
    return pl.pallas_call(
        _norm_router_kernel,
        grid=(n // tm,),
        in_specs=[pl.BlockSpec((tm, D_MODEL), lambda i: (i, 0)),
                  pl.BlockSpec((1, D_MODEL), lambda i: (0, 0)),
                  pl.BlockSpec((1, 1, D_MODEL), lambda i: (seg_of(i), 0, 0)),
                  pl.BlockSpec((1, 1, D_MODEL), lambda i: (seg_of(i), 0, 0)),
                  pl.BlockSpec((N_EXPERTS, D_MODEL), lambda i: (0, 0)),
                  pl.BlockSpec((N_EXPERTS, 1), lambda i: (0, 0))],
        out_specs=[pl.BlockSpec((tm, D_MODEL), lambda i: (i, 0)),
                   pl.BlockSpec((TOP_K, tm), lambda i: (0, i)),
                   pl.BlockSpec((TOP_K, tm), lambda i: (0, i))],
        out_shape=[jax.ShapeDtypeStruct((n, D_MODEL), F32),
                   jax.ShapeDtypeStruct((TOP_K, n), jnp.int32),
                   jax.ShapeDtypeStruct((TOP_K, n), F32)],
        compiler_params=_params(),
        name="norm_router",
    )(x, g, sc, sh, router_w.T, router_b.reshape(-1, 1))


def moe_routing(top_idx_t, n_tiles):
    n = top_idx_t.shape[1]
    experts = jnp.arange(N_EXPERTS, dtype=jnp.int32)
    onehot = (top_idx_t[:, :, None] == experts).astype(jnp.int32)
    per_tok = jnp.sum(onehot, axis=0)
    rank = jnp.cumsum(per_tok, axis=0) - per_tok
    counts = jnp.sum(per_tok, axis=0)
    padded = (counts + MOE_BLOCK - 1) // MOE_BLOCK * MOE_BLOCK
    padded_end = jnp.cumsum(padded)
    padded_start = padded_end - padded
    dest = jnp.sum(onehot * (rank + padded_start)[None], axis=-1)
    n_rows = n_tiles * MOE_BLOCK
    tok = jnp.broadcast_to(jnp.arange(n, dtype=jnp.int32)[None], dest.shape)
    row_tok = jnp.zeros((n_rows,), jnp.int32).at[dest.reshape(-1)].set(tok.reshape(-1), unique_indices=True)
    n_used = (padded_end[-1] // MOE_BLOCK).astype(jnp.int32)
    tile_start = jnp.arange(n_tiles, dtype=jnp.int32) * MOE_BLOCK
    tile_e = jnp.minimum(jnp.searchsorted(padded_end, tile_start, side='right'), N_EXPERTS - 1).astype(jnp.int32)
    last_e = tile_e[jnp.maximum(n_used - 1, 0)]
    tile_e = jnp.where(jnp.arange(n_tiles) < n_used, tile_e, last_e)
    return dest, row_tok, tile_e, n_used.reshape(1)


def _row_gather(src_hbm, dst_vmem, idx_ref, n_rows, sem, idx_at):
    def issue(r, carry):
        pltpu.make_async_copy(src_hbm.at[pl.ds(idx_at(idx_ref, r), 1)], dst_vmem.at[pl.ds(r, 1)], sem).start()
        return carry
    lax.fori_loop(0, n_rows, issue, 0)
    pltpu.make_async_copy(src_hbm.at[pl.ds(0, n_rows)], dst_vmem, sem).wait()


def _expert_kernel(tile_e_ref, n_used_ref, rows_ref, h_hbm, wgu_ref, bgu_ref, wd_ref, bd_ref, y_ref, xbuf, sem):
    t = pl.program_id(0)

    @pl.when(t < n_used_ref[0])
    def _():
        _row_gather(h_hbm, xbuf, rows_ref, MOE_BLOCK, sem, lambda ref, r: ref[0, 0, r])
        x = xbuf[...].astype(BF16)
        gu = _dot(x, wgu_ref[0]) + bgu_ref[0]
        d_ff = gu.shape[1] // 2
        g_ = jnp.minimum(gu[:, :d_ff], SWIGLU_LIMIT)
        u_ = jnp.clip(gu[:, d_ff:], -SWIGLU_LIMIT, SWIGLU_LIMIT)
        hh = g_ * jax.nn.sigmoid(SWIGLU_ALPHA * g_) * (u_ + 1.0)
        y_ref[...] = _dot(hh.astype(BF16), wd_ref[0]) + bd_ref[0]

    @pl.when(t >= n_used_ref[0])
    def _():
        y_ref[...] = jnp.zeros_like(y_ref)


def moe_experts(h, row_tok, tile_e, n_used, w_gu, b_gu, w_down, b_down):
    n_tiles = tile_e.shape[0]
    d_ff2 = w_gu.shape[2]
    grid_spec = pltpu.PrefetchScalarGridSpec(
        num_scalar_prefetch=2,
        grid=(n_tiles,),
        in_specs=[pl.BlockSpec((1, 1, MOE_BLOCK), lambda t, te, nu: (t, 0, 0), memory_space=pltpu.SMEM),
                  pl.BlockSpec(memory_space=pl.ANY),
                  pl.BlockSpec((1, D_MODEL, d_ff2), lambda t, te, nu: (te[t], 0, 0)),
                  pl.BlockSpec((1, 1, d_ff2), lambda t, te, nu: (te[t], 0, 0)),
                  pl.BlockSpec((1, d_ff2 // 2, D_MODEL), lambda t, te, nu: (te[t], 0, 0)),
                  pl.BlockSpec((1, 1, D_MODEL), lambda t, te, nu: (te[t], 0, 0))],
        out_specs=pl.BlockSpec((MOE_BLOCK, D_MODEL), lambda t, te, nu: (t, 0)),
        scratch_shapes=[pltpu.VMEM((MOE_BLOCK, D_MODEL), F32), pltpu.SemaphoreType.DMA(())],
    )
    return pl.pallas_call(
        _expert_kernel,
        grid_spec=grid_spec,
        out_shape=jax.ShapeDtypeStruct((n_tiles * MOE_BLOCK, D_MODEL), F32),
        compiler_params=_params(),
        name="moe_experts",
    )(tile_e, n_used, row_tok.reshape(n_tiles, 1, MOE_BLOCK), h, w_gu, b_gu, w_down, b_down)


def _combine_kernel(dest_ref, y_hbm, x_ref, gate_ref, g2_ref, o_ref, ybuf, sem, *, tm):
    _row_gather(y_hbm, ybuf, dest_ref, TOP_K * tm, sem, lambda ref, r: ref[0, 0, r])
    gate = gate_ref[...]
    acc = gate[:, 0:1] * ybuf[0:tm, :]
    for k in range(1, TOP_K):
        acc = acc + gate[:, k:k + 1] * ybuf[k * tm:(k + 1) * tm, :]
    o_ref[...] = x_ref[...] + g2_ref[0] * acc


def moe_combine(x, y_rows, dest, gate, g2, segs, tm=128):
    n = x.shape[0]
    nt = n // tm
    dest_tiles = dest.reshape(TOP_K, nt, tm).transpose(1, 0, 2).reshape(nt, 1, TOP_K * tm)
    seg_of = lambda i: (i * tm) // segs.seg
    return pl.pallas_call(
        functools.partial(_combine_kernel, tm=tm),
        grid=(nt,),
        in_specs=[pl.BlockSpec((1, 1, TOP_K * tm), lambda i: (i, 0, 0), memory_space=pltpu.SMEM),
                  pl.BlockSpec(memory_space=pl.ANY),
                  pl.BlockSpec((tm, D_MODEL), lambda i: (i, 0)),
                  pl.BlockSpec((tm, TOP_K), lambda i: (i, 0)),
                  pl.BlockSpec((1, 1, D_MODEL), lambda i: (seg_of(i), 0, 0))],
        out_specs=pl.BlockSpec((tm, D_MODEL), lambda i: (i, 0)),
        out_shape=jax.ShapeDtypeStruct((n, D_MODEL), F32),
        scratch_shapes=[pltpu.VMEM((TOP_K * tm, D_MODEL), F32), pltpu.SemaphoreType.DMA(())],
        compiler_params=_params(),
        name="moe_combine",
    )(dest_tiles, y_rows, x, gate, g2)


def moe_layer(x, g, sc, sh, g2, router_w, router_b, w_gu, b_gu, w_down, b_down, segs):
    n = x.shape[0]
    n_tiles = -(-(n * TOP_K + N_EXPERTS * (MOE_BLOCK - 1)) // MOE_BLOCK)
    h, top_idx_t, gate_t = norm_router(x, g, sc, sh, router_w, router_b, segs)
    dest, row_tok, tile_e, n_used = moe_routing(top_idx_t, n_tiles)
    y_rows = moe_experts(h, row_tok, tile_e, n_used, w_gu, b_gu, w_down, b_down)
    return moe_combine(x, y_rows, dest, gate_t.T, g2, segs)


def _pad_cols(w, width):
    return jnp.pad(w, ((0, 0), (0, width - w.shape[1])))


def trunk(x, c, p, segs):
    nseg = segs.nseg
    rows = -(-nseg // 8) * 8
    c_pad = jnp.pad(c, ((0, rows - nseg), (0, 0)))
    mod = ada_modulation(c_pad, p['ada_w'], p['ada_b'])[:, :nseg].reshape(DEPTH, nseg, 6, 1, D_MODEL)

    for layer in range(DEPTH):
        sh1, sc1, g1, sh2, sc2, g2 = [mod[layer, :, j] for j in range(6)]
        i = layer // 2
        n1 = p['norm1_g'][layer].reshape(1, -1)
        if layer % 2 == 0:
            w_in = p['ev_w_in'][i].astype(BF16)
            w_a = w_in[:, :1536]
            w_lr = _pad_cols(w_in[:, 1536:1568], 128)
            w_c = w_in[:, 1568:]
            a_proj, lr_proj, c_proj = norm_proj(x, n1, sc1, sh1, [w_a, w_lr, w_c], segs)
            o_f, o_b = gla_bidir(a_proj, lr_proj, p['gla_wa_f'][i], p['gla_ba_f'][i],
                                 p['gla_wa_b'][i], p['gla_ba_b'][i], segs)
            x = even_out(x, o_f, o_b, a_proj, c_proj, p['gla_norm_g'][i], p['conv_w'][i],
                         p['ev_w_out'][i].astype(BF16), g1, segs)
        else:
            w_in = p['od_w_in'][i].astype(BF16)
            q_proj, kv_proj, du = norm_proj(x, n1, sc1, sh1, [w_in[:, :512], w_in[:, 512:768], w_in[:, 768:]], segs)
            att = windowed_attention(q_proj, kv_proj, p['q_norm_g'][i], p['k_norm_g'][i], p['attn_sink'][i], segs)
            tables = s5_tables(p['s5_lam_re'][i], p['s5_lam_im'][i], p['s5_log_step'][i],
                               p['s5_b_re'][i], p['s5_b_im'][i], p['s5_c_re'][i], p['s5_c_im'][i])
            ys = s5_conv(du, tables, segs)
            x = odd_out(x, att, du, ys, p['s5_d'][i], p['s5_glu_w'][i].astype(BF16), p['s5_glu_b'][i],
                        p['od_w_out'][i].astype(BF16), g1, segs)
        x = moe_layer(x, p['norm2_g'][layer].reshape(1, -1), sc2, sh2, g2,
                      p['router_w'][layer], p['router_b'][layer],
                      p['exp_w_gu'][layer].astype(BF16), p['exp_b_gu'][layer].reshape(N_EXPERTS, 1, -1),
                      p['exp_w_down'][layer].astype(BF16), p['exp_b_down'][layer].reshape(N_EXPERTS, 1, -1), segs)
    return x


def kernel(x_prompt, x_sample, c_prompt, c_sample, ada_w, ada_b, norm1_g, norm2_g, ev_w_in, ev_w_out, gla_wa_f, gla_ba_f, gla_wa_b, gla_ba_b, gla_norm_g, conv_w, od_w_in, od_w_out, q_norm_g, k_norm_g, attn_sink, s5_lam_re, s5_lam_im, s5_log_step, s5_b_re, s5_b_im, s5_c_re, s5_c_im, s5_d, s5_glu_w, s5_glu_b, router_w, router_b, exp_w_gu, exp_b_gu, exp_w_down, exp_b_down):
    p = dict(ada_w=ada_w, ada_b=ada_b, norm1_g=norm1_g, norm2_g=norm2_g,
             ev_w_in=ev_w_in, ev_w_out=ev_w_out, gla_wa_f=gla_wa_f, gla_ba_f=gla_ba_f,
             gla_wa_b=gla_wa_b, gla_ba_b=gla_ba_b, gla_norm_g=gla_norm_g, conv_w=conv_w,
             od_w_in=od_w_in, od_w_out=od_w_out, q_norm_g=q_norm_g, k_norm_g=k_norm_g,
             attn_sink=attn_sink, s5_lam_re=s5_lam_re, s5_lam_im=s5_lam_im, s5_log_step=s5_log_step,
             s5_b_re=s5_b_re, s5_b_im=s5_b_im, s5_c_re=s5_c_re, s5_c_im=s5_c_im, s5_d=s5_d,
             s5_glu_w=s5_glu_w, s5_glu_b=s5_glu_b, router_w=router_w, router_b=router_b,
             exp_w_gu=exp_w_gu, exp_b_gu=exp_b_gu, exp_w_down=exp_w_down, exp_b_down=exp_b_down)
    bp, tp, _ = x_prompt.shape
    bs, ts, _ = x_sample.shape
    seg = math.gcd(tp, ts)
    per_p, per_s = tp // seg, ts // seg
    assert bs == 1 and per_p == 1 and per_s == 2, "segment layout: prompt sequences of one segment, one sample sequence of two"
    nseg = bp * per_p + bs * per_s
    segs = Segs(seg=seg, nseg=nseg, join=bp * per_p + 1)
    x = jnp.concatenate([x_prompt.reshape(-1, D_MODEL), x_sample.reshape(-1, D_MODEL)], axis=0)
    c = jnp.concatenate([c_prompt, jnp.repeat(c_sample, per_s, axis=0)], axis=0)
    y = trunk(x, c, p, segs)
    n_p = bp * tp
    return y[:n_p].reshape(x_prompt.shape), y[n_p:].reshape(x_sample.shape)
```

```python
import functools
import math
from typing import NamedTuple

import numpy as np
import jax
import jax.numpy as jnp
from jax import lax
from jax.experimental import pallas as pl
from jax.experimental.pallas import tpu as pltpu

F32 = jnp.float32
BF16 = jnp.bfloat16

D_MODEL = 1024
DEPTH = 4
HALF = 512
HEAD_DIM = 64

GLA_HEADS = 4
GLA_DV = 128
GLA_DK = 64
GLA_RANK = 16
GLA_TAU = 16.0
GLA_CHUNK = 64

CONV_WIDTH = 3

ATT_Q_HEADS = 8
ATT_KV_HEADS = 2
ATT_WINDOW = 128

S5_GROUP = 16
S5_GROUPS = 32
S5_STATE = 64
S5_CHUNK = 16

N_EXPERTS = 32
TOP_K = 4
SWIGLU_LIMIT = 7.0
SWIGLU_ALPHA = 1.702
MOE_BLOCK = 256

NORM_EPS = 1e-6
NEG_BIG = -1e30

LANES = 128
ROW_SLABS = D_MODEL // LANES
GATHER_UNROLL = 8

VMEM_LIMIT_BYTES = 52 * 1024 * 1024

NT_DIMS = (((1,), (1,)), ((), ()))
TN_DIMS = (((0,), (0,)), ((), ()))


class Segs(NamedTuple):
    seg: int
    nseg: int
    join: int


def _params(n_axes=1):
    return pltpu.CompilerParams(dimension_semantics=("arbitrary",) * n_axes,
                                vmem_limit_bytes=VMEM_LIMIT_BYTES)


def _split_bf16(a):
    hi = a.astype(BF16)
    lo = (a - hi.astype(F32)).astype(BF16)
    return hi, lo


def _dot(a, b):
    return jnp.dot(a, b, preferred_element_type=F32)


def _dot_split(a, b):
    a_hi, a_lo = _split_bf16(a)
    b_hi, b_lo = _split_bf16(b)
    return _dot(a_hi, b_hi) + _dot(a_lo, b_hi) + _dot(a_hi, b_lo)


def _store_slabs(ref, value):
    for s in range(ROW_SLABS):
        ref[:, s, :] = value[:, s * LANES:(s + 1) * LANES]


def _rms_mod(x, g, sc, sh):
    ms = jnp.mean(x * x, axis=-1, keepdims=True)
    h = x * lax.rsqrt(ms + NORM_EPS) * g
    return h * (1.0 + sc) + sh


def _ada_kernel(c_ref, w_ref, b_ref, o_ref):
    c = c_ref[...]
    s = c * jax.nn.sigmoid(c)
    o_ref[0] = _dot_split(s, w_ref[0]) + b_ref[0]


def ada_modulation(c_pad, ada_w, ada_b):
    rows = c_pad.shape[0]
    cols = 6 * D_MODEL
    tn = 1536
    return pl.pallas_call(
        _ada_kernel,
        grid=(DEPTH, cols // tn),
        in_specs=[pl.BlockSpec((rows, D_MODEL), lambda l, j: (0, 0)),
                  pl.BlockSpec((1, D_MODEL, tn), lambda l, j: (l, 0, j)),
                  pl.BlockSpec((1, 1, tn), lambda l, j: (l, 0, j))],
        out_specs=pl.BlockSpec((1, rows, tn), lambda l, j: (l, 0, j)),
        out_shape=jax.ShapeDtypeStruct((DEPTH, rows, cols), F32),
        compiler_params=_params(2),
        name="ada_modulation",
    )(c_pad, ada_w, ada_b.reshape(DEPTH, 1, cols))


def _norm_proj_kernel(x_ref, g_ref, sc_ref, sh_ref, *refs, n_w):
    h = _rms_mod(x_ref[...], g_ref[...], sc_ref[0], sh_ref[0]).astype(BF16)
    for w_ref, o_ref in zip(refs[:n_w], refs[n_w:]):
        o_ref[...] = _dot(h, w_ref[...]).astype(o_ref.dtype)


def norm_proj(x, g, sc, sh, weights, segs, tm=512):
    n = x.shape[0]
    n_w = len(weights)
    seg_of = lambda i: (i * tm) // segs.seg
    in_specs = [pl.BlockSpec((tm, D_MODEL), lambda i: (i, 0)),
                pl.BlockSpec((1, D_MODEL), lambda i: (0, 0)),
                pl.BlockSpec((1, 1, D_MODEL), lambda i: (seg_of(i), 0, 0)),
                pl.BlockSpec((1, 1, D_MODEL), lambda i: (seg_of(i), 0, 0))]
    in_specs += [pl.BlockSpec(w.shape, lambda i: (0, 0)) for w in weights]
    return pl.pallas_call(
        functools.partial(_norm_proj_kernel, n_w=n_w),
        grid=(n // tm,),
        in_specs=in_specs,
        out_specs=[pl.BlockSpec((tm, w.shape[1]), lambda i: (i, 0)) for w in weights],
        out_shape=[jax.ShapeDtypeStruct((n, w.shape[1]), F32) for w in weights],
        compiler_params=_params(),
        name="norm_proj",
    )(x, g, sc, sh, *weights)


def _gla_kernel(qk_f, v_f, lr_f, qk_b, v_b, lr_b, wa_f, ba_f, wa_b, ba_b,
                of_ref, ob_ref, st_f, st_b, *, segs, rows):
    i = pl.program_id(0)
    nb = pl.num_programs(0)
    ib = nb - 1 - i
    bps = segs.seg // rows
    seg_f = i // bps
    seg_b = ib // bps

    @pl.when((i % bps == 0) & (seg_f != segs.join))
    def _():
        st_f[...] = jnp.zeros_like(st_f)

    @pl.when((ib % bps == bps - 1) & (seg_b + 1 != segs.join))
    def _():
        st_b[...] = jnp.zeros_like(st_b)

    L = GLA_CHUNK
    r_io = lax.broadcasted_iota(jnp.int32, (L, L), 0)
    c_io = lax.broadcasted_iota(jnp.int32, (L, L), 1)
    lower = r_io >= c_io
    upper = r_io <= c_io
    scale = GLA_DK ** -0.5
    n_chunks = rows // L

    def one_chunk(c, qk_ref, v_ref, lr_ref, wa_ref, ba_ref, st_ref, o_ref, mask, edge_row):
        r0 = pl.multiple_of(c * L, L)
        qk = qk_ref[pl.ds(r0, L), :]
        q = qk[:, :GLA_HEADS * GLA_DK] * scale
        k = qk[:, GLA_HEADS * GLA_DK:]
        v = v_ref[pl.ds(r0, L), :]
        pre = _dot_split(lr_ref[pl.ds(r0, L), :], wa_ref[...]) + ba_ref[...]
        log_sig = jnp.minimum(pre, 0.0) - jnp.log(1.0 + jnp.exp(-jnp.abs(pre)))
        g = log_sig * (1.0 / GLA_TAU)
        g_hi, g_lo = _split_bf16(g)
        tri = mask.astype(BF16)
        b = _dot(tri, g_hi) + _dot(tri, g_lo)
        b_edge = b[edge_row:edge_row + 1, :]
        qe = q * jnp.exp(b)
        ke = k * jnp.exp(-b)
        kd = k * jnp.exp(b_edge - b)
        d_edge = jnp.exp(b_edge)
        outs = []
        for h in range(GLA_HEADS):
            ks = slice(h * GLA_DK, (h + 1) * GLA_DK)
            vs = slice(h * GLA_DV, (h + 1) * GLA_DV)
            qeh = qe[:, ks].astype(BF16)
            keh = ke[:, ks].astype(BF16)
            kdh = kd[:, ks].astype(BF16)
            vh = v[:, vs].astype(BF16)
            st = st_ref[h]
            att = lax.dot_general(qeh, keh, NT_DIMS, preferred_element_type=F32)
            att = jnp.where(mask, att, 0.0).astype(BF16)
            o = lax.dot_general(qeh, st.astype(BF16), NT_DIMS, preferred_element_type=F32)
            o = o + _dot(att, vh)
            st_ref[h] = st * d_edge[:, ks] + lax.dot_general(vh, kdh, TN_DIMS, preferred_element_type=F32)
            outs.append(o)
        o_ref[pl.ds(r0, L), :] = jnp.concatenate(outs, axis=1)

    def body(c, carry):
        one_chunk(c, qk_f, v_f, lr_f, wa_f, ba_f, st_f, of_ref, lower, L - 1)
        one_chunk(n_chunks - 1 - c, qk_b, v_b, lr_b, wa_b, ba_b, st_b, ob_ref, upper, 0)
        return carry

    lax.fori_loop(0, n_chunks, body, 0, unroll=2)


def gla_bidir(a_proj, lr_proj, wa_f, ba_f, wa_b, ba_b, segs, rows=512):
    n = a_proj.shape[0]
    nb = n // rows
    lanes = lr_proj.shape[1]
    wa_f_pad = jnp.zeros((lanes, GLA_HEADS * GLA_DK), F32).at[:GLA_RANK].set(wa_f)
    wa_b_pad = jnp.zeros((lanes, GLA_HEADS * GLA_DK), F32).at[GLA_RANK:2 * GLA_RANK].set(wa_b)
    fwd = lambda i: (i, 0)
    bwd = lambda i: (nb - 1 - i, 0)
    const = lambda i: (0, 0)
    spec = lambda width, col, row_map: pl.BlockSpec((rows, width), lambda i: (row_map(i)[0], col))
    return pl.pallas_call(
        functools.partial(_gla_kernel, segs=segs, rows=rows),
        grid=(nb,),
        in_specs=[spec(512, 0, fwd), spec(512, 1, fwd), pl.BlockSpec((rows, lanes), fwd),
                  spec(512, 0, bwd), spec(512, 1, bwd), pl.BlockSpec((rows, lanes), bwd),
                  pl.BlockSpec(wa_f_pad.shape, const), pl.BlockSpec((1, 256), const),
                  pl.BlockSpec(wa_b_pad.shape, const), pl.BlockSpec((1, 256), const)],
        out_specs=[pl.BlockSpec((rows, HALF), fwd), pl.BlockSpec((rows, HALF), bwd)],
        out_shape=[jax.ShapeDtypeStruct((n, HALF), F32)] * 2,
        scratch_shapes=[pltpu.VMEM((GLA_HEADS, GLA_DV, GLA_DK), F32)] * 2,
        compiler_params=_params(),
        name="gla_bidir",
    )(a_proj, a_proj, lr_proj, a_proj, a_proj, lr_proj,
      wa_f_pad, ba_f.reshape(1, -1), wa_b_pad, ba_b.reshape(1, -1))


def _even_out_kernel(x_ref, of_ref, ob_ref, og_ref, c_ref, cprev_ref, cnext_ref, ng_ref, cw_ref,
                     w_ref, g1_ref, o_ref, *, segs, tm):
    i = pl.program_id(0)
    tps = segs.seg // tm
    seg = i // tps
    has_prev = jnp.logical_not((i % tps == 0) & (seg != segs.join))
    has_next = jnp.logical_not((i % tps == tps - 1) & (seg + 1 != segs.join))

    o = of_ref[...] + ob_ref[...]
    og = og_ref[...]
    parts = []
    for h in range(GLA_HEADS):
        oh = o[:, h * GLA_DV:(h + 1) * GLA_DV]
        ms = jnp.mean(oh * oh, axis=-1, keepdims=True)
        parts.append(oh * lax.rsqrt(ms + NORM_EPS) * ng_ref[...])
    a_out = jnp.concatenate(parts, axis=1) * (og * jax.nn.sigmoid(og))

    c = c_ref[...]
    bg = c[:, :HALF]
    u = c[:, HALF:2 * HALF] * c[:, 2 * HALF:]
    cp = cprev_ref[7:8, :]
    cn = cnext_ref[0:1, :]
    u_prev_edge = jnp.where(has_prev, cp[:, HALF:2 * HALF] * cp[:, 2 * HALF:], 0.0)
    u_next_edge = jnp.where(has_next, cn[:, HALF:2 * HALF] * cn[:, 2 * HALF:], 0.0)
    row = lax.broadcasted_iota(jnp.int32, (tm, 1), 0)
    u_prev = jnp.where(row == 0, u_prev_edge, pltpu.roll(u, 1, axis=0))
    u_next = jnp.where(row == tm - 1, u_next_edge, pltpu.roll(u, tm - 1, axis=0))
    cw = cw_ref[...]
    b_out = bg * (cw[0:1, :] * u_prev + cw[1:2, :] * u + cw[2:3, :] * u_next)

    w = w_ref[...]
    mix = _dot(a_out.astype(BF16), w[:HALF, :]) + _dot(b_out.astype(BF16), w[HALF:, :])
    o_ref[...] = x_ref[...] + g1_ref[0] * mix


def even_out(x, o_f, o_b, a_proj, c_proj, norm_g, conv_w, w_out, g1, segs, tm=512):
    n = x.shape[0]
    last8 = n // 8 - 1
    seg_of = lambda i: (i * tm) // segs.seg
    return pl.pallas_call(
        functools.partial(_even_out_kernel, segs=segs, tm=tm),
        grid=(n // tm,),
        in_specs=[pl.BlockSpec((tm, D_MODEL), lambda i: (i, 0)),
                  pl.BlockSpec((tm, HALF), lambda i: (i, 0)),
                  pl.BlockSpec((tm, HALF), lambda i: (i, 0)),
                  pl.BlockSpec((tm, HALF), lambda i: (i, 2)),
                  pl.BlockSpec((tm, 3 * HALF), lambda i: (i, 0)),
                  pl.BlockSpec((8, 3 * HALF), lambda i: (jnp.maximum(i * (tm // 8) - 1, 0), 0)),
                  pl.BlockSpec((8, 3 * HALF), lambda i: (jnp.minimum((i + 1) * (tm // 8), last8), 0)),
                  pl.BlockSpec((1, GLA_DV), lambda i: (0, 0)),
                  pl.BlockSpec((CONV_WIDTH, HALF), lambda i: (0, 0)),
                  pl.BlockSpec((D_MODEL, D_MODEL), lambda i: (0, 0)),
                  pl.BlockSpec((1, 1, D_MODEL), lambda i: (seg_of(i), 0, 0))],
        out_specs=pl.BlockSpec((tm, D_MODEL), lambda i: (i, 0)),
        out_shape=jax.ShapeDtypeStruct((n, D_MODEL), F32),
        compiler_params=_params(),
        name="even_out",
    )(x, o_f, o_b, a_proj, c_proj, c_proj, c_proj, norm_g.reshape(1, -1), conv_w, w_out, g1)


def _attn_kernel(q_ref, kv_ref, kvp_ref, kvn_ref, qg_ref, kg_ref, sink_ref, o_ref, *, segs, tq):
    i = pl.program_id(0)
    W = ATT_WINDOW
    tps = segs.seg // tq
    seg = i // tps
    first = (i % tps == 0) & (seg != segs.join)
    last = (i % tps == tps - 1) & (seg + 1 != segs.join)

    kv_all = jnp.concatenate([kvp_ref[...], kv_ref[...], kvn_ref[...]], axis=0)
    kvw = ATT_KV_HEADS * HEAD_DIM
    k_heads, v_heads = [], []
    for h in range(ATT_KV_HEADS):
        kh = kv_all[:, h * HEAD_DIM:(h + 1) * HEAD_DIM]
        ms = jnp.mean(kh * kh, axis=-1, keepdims=True)
        k_heads.append((kh * lax.rsqrt(ms + NORM_EPS) * kg_ref[...]).astype(BF16))
        v_heads.append(kv_all[:, kvw + h * HEAD_DIM:kvw + (h + 1) * HEAD_DIM].astype(BF16))

    t_io = lax.broadcasted_iota(jnp.int32, (W, 3 * W), 0)
    j_io = lax.broadcasted_iota(jnp.int32, (W, 3 * W), 1)
    rel = j_io - W - t_io
    dist = jnp.abs(rel)
    in_window = dist <= W
    dist_f = dist.astype(F32)
    group = ATT_Q_HEADS // ATT_KV_HEADS
    n_blk = tq // W
    for blk in range(n_blk):
        valid = in_window
        if blk == 0:
            valid = valid & ((j_io >= W) | jnp.logical_not(first))
        if blk == n_blk - 1:
            valid = valid & ((j_io < 2 * W) | jnp.logical_not(last))
        q = q_ref[blk * W:(blk + 1) * W, :]
        outs = []
        for hq in range(ATT_Q_HEADS):
            kvh = hq // group
            qh = q[:, hq * HEAD_DIM:(hq + 1) * HEAD_DIM]
            ms = jnp.mean(qh * qh, axis=-1, keepdims=True)
            qn = (qh * lax.rsqrt(ms + NORM_EPS) * qg_ref[...] * (HEAD_DIM ** -0.5)).astype(BF16)
            kh = k_heads[kvh][blk * W:(blk + 3) * W, :]
            vh = v_heads[kvh][blk * W:(blk + 3) * W, :]
            s = lax.dot_general(qn, kh, NT_DIMS, preferred_element_type=F32)
            slope = 2.0 ** (-8.0 * (hq + 1) / ATT_Q_HEADS)
            s = jnp.where(valid, s - slope * dist_f, NEG_BIG)
            sk = sink_ref[hq]
            m = jnp.maximum(jnp.max(s, axis=-1, keepdims=True), sk)
            p = jnp.exp(s - m)
            denom = jnp.sum(p, axis=-1, keepdims=True) + jnp.exp(sk - m)
            outs.append(_dot(p.astype(BF16), vh) / denom)
        o_ref[blk * W:(blk + 1) * W, :] = jnp.concatenate(outs, axis=1)


def windowed_attention(q_proj, kv_proj, q_norm_g, k_norm_g, sink, segs, tq=512):
    n = q_proj.shape[0]
    W = ATT_WINDOW
    r = tq // W
    last = n // W - 1
    kvw = 2 * ATT_KV_HEADS * HEAD_DIM
    return pl.pallas_call(
        functools.partial(_attn_kernel, segs=segs, tq=tq),
        grid=(n // tq,),
        in_specs=[pl.BlockSpec((tq, HALF), lambda i: (i, 0)),
                  pl.BlockSpec((tq, kvw), lambda i: (i, 0)),
                  pl.BlockSpec((W, kvw), lambda i: (jnp.maximum(i * r - 1, 0), 0)),
                  pl.BlockSpec((W, kvw), lambda i: (jnp.minimum((i + 1) * r, last), 0)),
                  pl.BlockSpec((1, HEAD_DIM), lambda i: (0, 0)),
                  pl.BlockSpec((1, HEAD_DIM), lambda i: (0, 0)),
                  pl.BlockSpec(memory_space=pltpu.SMEM)],
        out_specs=pl.BlockSpec((tq, HALF), lambda i: (i, 0)),
        out_shape=jax.ShapeDtypeStruct((n, HALF), F32),
        compiler_params=_params(),
        name="windowed_attention",
    )(q_proj, kv_proj, kv_proj, kv_proj, q_norm_g.reshape(1, -1), k_norm_g.reshape(1, -1), sink)


def s5_tables(lam_re, lam_im, log_step, b_re, b_im, c_re, c_im, *, n_inner):
    L = S5_CHUNK
    dt = jnp.exp(log_step)[:, :, None]
    lr, li = lam_re, lam_im
    mag = jnp.exp(lr * dt)
    ar, ai = mag * jnp.cos(li * dt), mag * jnp.sin(li * dt)
    den = lr * lr + li * li
    zr = ((ar - 1.0) * lr + ai * li) / den
    zi = (ai * lr - (ar - 1.0) * li) / den
    bbr = zr[..., None] * b_re - zi[..., None] * b_im
    bbi = zr[..., None] * b_im + zi[..., None] * b_re
    tau = jnp.arange(L + 1, dtype=F32)[:, None, None, None]
    pmag = jnp.exp(lr[None] * dt[None] * tau)
    pang = li[None] * dt[None] * tau
    pr, pi = pmag * jnp.cos(pang), pmag * jnp.sin(pang)

    hp = lax.Precision.HIGHEST
    car = c_re[None] * pr[:, :, :, None, :] - c_im[None] * pi[:, :, :, None, :]
    cai = c_re[None] * pi[:, :, :, None, :] + c_im[None] * pr[:, :, :, None, :]
    kern = (jnp.einsum('ldgop,dgpi->ldgoi', car, bbr, precision=hp)
            - jnp.einsum('ldgop,dgpi->ldgoi', cai, bbi, precision=hp))
    s_idx = jnp.arange(L)[:, None]
    t_idx = jnp.arange(L)[None, :]
    lag_f = jnp.clip(t_idx - s_idx, 0, L)
    lag_b = jnp.clip(s_idx - t_idx, 0, L)
    kf = jnp.where((t_idx >= s_idx)[:, :, None, None, None], kern[:, 0][lag_f], 0.0)
    kb = jnp.where((s_idx >= t_idx)[:, :, None, None, None], kern[:, 1][lag_b], 0.0)
    m = (kf + kb).transpose(2, 0, 4, 1, 3).reshape(S5_GROUPS, L * S5_GROUP, L * S5_GROUP)

    abr = pr[..., None] * bbr[None] - pi[..., None] * bbi[None]
    abi = pr[..., None] * bbi[None] + pi[..., None] * bbr[None]
    e_f = L - 1 - jnp.arange(L)
    e_b = jnp.arange(L)
    to_rows = lambda a: a.transpose(1, 0, 3, 2).reshape(S5_GROUPS, L * S5_GROUP, S5_STATE)
    wp = jnp.concatenate([to_rows(abr[e_f, 0]), to_rows(abi[e_f, 0]),
                          to_rows(abr[e_b, 1]), to_rows(abi[e_b, 1])], axis=-1)

    o_f = jnp.arange(L) + 1
    o_b = L - jnp.arange(L)
    to_cols = lambda a: a.transpose(1, 3, 0, 2).reshape(S5_GROUPS, S5_STATE, L * S5_GROUP)
    wc = jnp.concatenate([to_cols(car[o_f, 0]), -to_cols(cai[o_f, 0]),
                          to_cols(car[o_b, 1]), -to_cols(cai[o_b, 1])], axis=1)

    steps = jnp.arange(n_inner + 1, dtype=F32)[:, None, None, None] * float(L)
    qmag = jnp.exp(lr[None] * dt[None] * steps)
    qang = li[None] * dt[None] * steps
    qr, qi = qmag * jnp.cos(qang), qmag * jnp.sin(qang)
    form_a = jnp.concatenate([qr, qr], -1)
    form_b = jnp.concatenate([-qi, qi], -1)
    dpow = jnp.stack([form_a[:, 0], form_b[:, 0], form_a[:, 1], form_b[:, 1]], axis=0)
    dpow = dpow.transpose(2, 0, 1, 3)
    return m.astype(BF16), wp.astype(BF16), wc, dpow


def _s5_kernel(x_ref, m_ref, wp_ref, wc_ref, dpow_ref, y_ref, pf_scr, pb_scr, sf_scr, sb_scr, *, segs, n_sub):
    P2 = 2 * S5_STATE
    n_inner = segs.seg // S5_CHUNK // n_sub
    rows_step = segs.nseg * n_sub
    x = x_ref[0]
    y_ref[0] = _dot(x, m_ref[0])
    p = _dot(x, wp_ref[0])
    pf_scr[...] = p[:, :P2]
    pb_scr[...] = p[:, P2:]

    def cmul(s, s_swapped, form_a, form_b):
        return s * form_a + s_swapped * form_b

    def swap(s):
        return pltpu.roll(s, S5_STATE, axis=1)

    def power(form, i):
        return dpow_ref[0, form, pl.ds(i, 1), :]

    d1 = [power(f, 1) for f in range(4)]
    dn = [power(f, n_inner) for f in range(4)]

    def block(i):
        return pl.ds(pl.multiple_of(i * rows_step, 8), rows_step)

    def local_step(i, carry):
        s_f, s_b = carry
        rf, rb = block(i), block(n_inner - 1 - i)
        sf_scr[rf, :] = s_f
        sb_scr[rb, :] = s_b
        return (cmul(s_f, swap(s_f), d1[0], d1[1]) + pf_scr[rf, :],
                cmul(s_b, swap(s_b), d1[2], d1[3]) + pb_scr[rb, :])
    zeros = jnp.zeros((rows_step, P2), F32)
    end_f, end_b = lax.fori_loop(0, n_inner, local_step, (zeros, zeros))

    r_io = lax.broadcasted_iota(jnp.int32, (rows_step, 1), 0)
    j_io, seg_io = r_io % n_sub, r_io // n_sub
    takes_prev = jnp.logical_not((j_io == 0) & (seg_io != segs.join))
    takes_next = jnp.logical_not((j_io == n_sub - 1) & (seg_io + 1 != segs.join))
    end_f_prev = pltpu.roll(end_f, 1, axis=0)
    end_b_next = pltpu.roll(end_b, rows_step - 1, axis=0)
    c_f, c_b = zeros, zeros
    longest = n_sub * (2 if segs.join >= 0 else 1)
    for _ in range(longest - 1):
        c_prev = pltpu.roll(c_f, 1, axis=0)
        c_f = jnp.where(takes_prev, cmul(c_prev, swap(c_prev), dn[0], dn[1]) + end_f_prev, 0.0)
        c_next = pltpu.roll(c_b, rows_step - 1, axis=0)
        c_b = jnp.where(takes_next, cmul(c_next, swap(c_next), dn[2], dn[3]) + end_b_next, 0.0)

    c_f_sw, c_b_sw = swap(c_f), swap(c_b)

    def fix_step(i, carry):
        rf, rb = block(i), block(n_inner - 1 - i)
        sf_scr[rf, :] += cmul(c_f, c_f_sw, power(0, i), power(1, i))
        sb_scr[rb, :] += cmul(c_b, c_b_sw, power(2, i), power(3, i))
        return carry
    lax.fori_loop(0, n_inner, fix_step, 0)

    wc_hi, wc_lo = _split_bf16(wc_ref[0])
    acc = y_ref[0]
    for s_scr, rows in ((sf_scr, slice(0, P2)), (sb_scr, slice(P2, 2 * P2))):
        s_hi, s_lo = _split_bf16(s_scr[...])
        acc = acc + _dot(s_hi, wc_hi[rows]) + _dot(s_lo, wc_hi[rows]) + _dot(s_hi, wc_lo[rows])
    y_ref[0] = acc


def s5_conv(du, params, segs, n_sub=8):
    n = du.shape[0]
    L = S5_CHUNK
    rows = n // L
    width = L * S5_GROUP
    n_inner = segs.seg // L // n_sub
    m, wp, wc, dpow = s5_tables(*params, n_inner=n_inner)
    xg = du.astype(BF16).reshape(segs.nseg, n_sub, n_inner, L, S5_GROUPS, S5_GROUP)
    xg = xg.transpose(4, 2, 0, 1, 3, 5).reshape(S5_GROUPS, rows, width)
    grp = lambda g: (g, 0, 0)
    yg = pl.pallas_call(
        functools.partial(_s5_kernel, segs=segs, n_sub=n_sub),
        grid=(S5_GROUPS,),
        in_specs=[pl.BlockSpec((1, rows, width), grp),
                  pl.BlockSpec((1, width, width), grp),
                  pl.BlockSpec((1, width, 4 * S5_STATE), grp),
                  pl.BlockSpec((1, 4 * S5_STATE, width), grp),
                  pl.BlockSpec((1, 4, n_inner + 1, 2 * S5_STATE), lambda g: (g, 0, 0, 0))],
        out_specs=pl.BlockSpec((1, rows, width), grp),
        out_shape=jax.ShapeDtypeStruct((S5_GROUPS, rows, width), F32),
        scratch_shapes=[pltpu.VMEM((rows, 2 * S5_STATE), F32)] * 4,
        compiler_params=_params(),
        name="s5_conv",
    )(xg, m, wp, wc, dpow)
    yg = yg.reshape(S5_GROUPS, n_inner, segs.nseg, n_sub, L, S5_GROUP)
    return yg.transpose(2, 3, 1, 4, 0, 5).reshape(n, HALF)


def _odd_out_kernel(x_ref, att_ref, du_ref, ys_ref, dsk_ref, gw_ref, gb_ref, w_ref, g1_ref, o_ref):
    y = dsk_ref[...] * du_ref[...] + ys_ref[...]
    z = 0.5 * y * (1.0 + jnp.tanh(math.sqrt(2.0 / math.pi) * (y + 0.044715 * (y * y * y))))
    gate = jax.nn.sigmoid(_dot(z.astype(BF16), gw_ref[...]) + gb_ref[...])
    d_out = z * gate
    w = w_ref[...]
    mix = _dot(att_ref[...].astype(BF16), w[:HALF, :]) + _dot(d_out.astype(BF16), w[HALF:, :])
    o_ref[...] = x_ref[...] + g1_ref[0] * mix


def odd_out(x, att, du, ys, d_skip, glu_w, glu_b, w_out, g1, segs, tm=512):
    n = x.shape[0]
    seg_of = lambda i: (i * tm) // segs.seg
    row = lambda i: (i, 0)
    const = lambda i: (0, 0)
    return pl.pallas_call(
        _odd_out_kernel,
        grid=(n // tm,),
        in_specs=[pl.BlockSpec((tm, D_MODEL), row), pl.BlockSpec((tm, HALF), row),
                  pl.BlockSpec((tm, HALF), row), pl.BlockSpec((tm, HALF), row),
                  pl.BlockSpec((1, HALF), const), pl.BlockSpec((HALF, HALF), const),
                  pl.BlockSpec((1, HALF), const), pl.BlockSpec((D_MODEL, D_MODEL), const),
                  pl.BlockSpec((1, 1, D_MODEL), lambda i: (seg_of(i), 0, 0))],
        out_specs=pl.BlockSpec((tm, D_MODEL), row),
        out_shape=jax.ShapeDtypeStruct((n, D_MODEL), F32),
        compiler_params=_params(),
        name="odd_out",
    )(x, att, du, ys, d_skip.reshape(1, -1), glu_w, glu_b.reshape(1, -1), w_out, g1)


def _norm_router_kernel(x_ref, g_ref, sc_ref, sh_ref, wr_ref, br_ref, h_ref, idx_ref, gate_ref):
    h = _rms_mod(x_ref[...], g_ref[...], sc_ref[0], sh_ref[0])
    _store_slabs(h_ref, h)
    h_hi, h_lo = _split_bf16(h)
    w_hi, w_lo = _split_bf16(wr_ref[...])
    nt = lambda a, b: lax.dot_general(a, b, NT_DIMS, preferred_element_type=F32)
    logits = nt(w_hi, h_hi) + nt(w_lo, h_hi) + nt(w_hi, h_lo) + br_ref[...]
    e_io = lax.broadcasted_iota(jnp.int32, logits.shape, 0)
    tops = []
    for k in range(TOP_K):
        m = jnp.max(logits, axis=0, keepdims=True)
        idx = jnp.min(jnp.where(logits == m, e_io, N_EXPERTS), axis=0, keepdims=True)
        idx_ref[k:k + 1, :] = idx
        logits = jnp.where(e_io == idx, -jnp.inf, logits)
        tops.append(m)
    es = [jnp.exp(t - tops[0]) for t in tops]
    total = es[0] + es[1] + es[2] + es[3]
    for k in range(TOP_K):
        gate_ref[k:k + 1, :] = es[k] / total


def norm_router(x, g, sc, sh, router_w, router_b, segs, tm=512):
    n = x.shape[0]
    seg_of = lambda i: (i * tm) // segs.seg
    return pl.pallas_call(
        _norm_router_kernel,
        grid=(n // tm,),
        in_specs=[pl.BlockSpec((tm, D_MODEL), lambda i: (i, 0)),
                  pl.BlockSpec((1, D_MODEL), lambda i: (0, 0)),
                  pl.BlockSpec((1, 1, D_MODEL), lambda i: (seg_of(i), 0, 0)),
                  pl.BlockSpec((1, 1, D_MODEL), lambda i: (seg_of(i), 0, 0)),
                  pl.BlockSpec((N_EXPERTS, D_MODEL), lambda i: (0, 0)),
                  pl.BlockSpec((N_EXPERTS, 1), lambda i: (0, 0))],
        out_specs=[pl.BlockSpec((tm, ROW_SLABS, LANES), lambda i: (i, 0, 0)),
                   pl.BlockSpec((TOP_K, tm), lambda i: (0, i)),
                   pl.BlockSpec((TOP_K, tm), lambda i: (0, i))],
        out_shape=[jax.ShapeDtypeStruct((n, ROW_SLABS, LANES), F32),
                   jax.ShapeDtypeStruct((TOP_K, n), jnp.int32),
                   jax.ShapeDtypeStruct((TOP_K, n), F32)],
        compiler_params=_params(),
        name="norm_router",
    )(x, g, sc, sh, router_w.T, router_b.reshape(-1, 1))


def moe_routing(top_idx_t, n_tiles):
    n = top_idx_t.shape[1]
    experts = jnp.arange(N_EXPERTS, dtype=jnp.int32)
    onehot = (top_idx_t[:, :, None] == experts).astype(jnp.int32)
    per_tok = jnp.sum(onehot, axis=0)
    rank = jnp.cumsum(per_tok, axis=0) - per_tok
    counts = jnp.sum(per_tok, axis=0)
    padded = (counts + MOE_BLOCK - 1) // MOE_BLOCK * MOE_BLOCK
    padded_end = jnp.cumsum(padded)
    padded_start = padded_end - padded
    dest = jnp.sum(onehot * (rank + padded_start)[None], axis=-1)
    n_rows = n_tiles * MOE_BLOCK
    tok = jnp.broadcast_to(jnp.arange(n, dtype=jnp.int32)[None], dest.shape)
    row_tok = jnp.zeros((n_rows,), jnp.int32).at[dest.reshape(-1)].set(tok.reshape(-1), unique_indices=True)
    n_used = (padded_end[-1] // MOE_BLOCK).astype(jnp.int32)
    tile_start = jnp.arange(n_tiles, dtype=jnp.int32) * MOE_BLOCK
    tile_e = jnp.minimum(jnp.searchsorted(padded_end, tile_start, side='right'), N_EXPERTS - 1).astype(jnp.int32)
    last_e = tile_e[jnp.maximum(n_used - 1, 0)]
    tile_e = jnp.where(jnp.arange(n_tiles) < n_used, tile_e, last_e)
    return dest, row_tok, tile_e, n_used.reshape(1)


def _start_row_gather(src_hbm, dst_vmem, sem, idx_ref, n_rows):
    def issue(r, carry):
        pltpu.make_async_copy(src_hbm.at[pl.ds(idx_ref[0, 0, r], 1)], dst_vmem.at[pl.ds(r, 1)], sem).start()
        return carry
    lax.fori_loop(0, n_rows, issue, 0, unroll=GATHER_UNROLL)


def _wait_row_gather(src_hbm, dst_vmem, sem, n_rows):
    pltpu.make_async_copy(src_hbm.at[pl.ds(0, n_rows)], dst_vmem, sem).wait()


def _expert_kernel(tile_e_ref, n_used_ref, rows_ref, rows_next_ref, h_hbm, wgu_ref, bgu_ref, wd_ref, bd_ref,
                   y_ref, xbuf, sems):
    t = pl.program_id(0)
    n_used = n_used_ref[0]
    slot = t % 2

    @pl.when((t == 0) & (n_used > 0))
    def _():
        _start_row_gather(h_hbm, xbuf.at[0], sems.at[0], rows_ref, MOE_BLOCK)

    @pl.when(t + 1 < n_used)
    def _():
        _start_row_gather(h_hbm, xbuf.at[1 - slot], sems.at[1 - slot], rows_next_ref, MOE_BLOCK)

    @pl.when(t < n_used)
    def _():
        _wait_row_gather(h_hbm, xbuf.at[slot], sems.at[slot], MOE_BLOCK)
        x = jnp.concatenate([xbuf[slot, :, s, :] for s in range(ROW_SLABS)], axis=1).astype(BF16)
        gu = _dot(x, wgu_ref[0]) + bgu_ref[0]
        d_ff = gu.shape[1] // 2
        g_ = jnp.minimum(gu[:, :d_ff], SWIGLU_LIMIT)
        u_ = jnp.clip(gu[:, d_ff:], -SWIGLU_LIMIT, SWIGLU_LIMIT)
        hh = g_ * jax.nn.sigmoid(SWIGLU_ALPHA * g_) * (u_ + 1.0)
        _store_slabs(y_ref, _dot(hh.astype(BF16), wd_ref[0]) + bd_ref[0])

    @pl.when(t >= n_used)
    def _():
        y_ref[...] = jnp.zeros_like(y_ref)


def moe_experts(h, row_tok, tile_e, n_used, w_gu, b_gu, w_down, b_down):
    n_tiles = tile_e.shape[0]
    d_ff2 = w_gu.shape[2]
    grid_spec = pltpu.PrefetchScalarGridSpec(
        num_scalar_prefetch=2,
        grid=(n_tiles,),
        in_specs=[pl.BlockSpec((1, 1, MOE_BLOCK), lambda t, te, nu: (t, 0, 0), memory_space=pltpu.SMEM),
                  pl.BlockSpec((1, 1, MOE_BLOCK), lambda t, te, nu: (jnp.minimum(t + 1, n_tiles - 1), 0, 0),
                               memory_space=pltpu.SMEM),
                  pl.BlockSpec(memory_space=pl.ANY),
                  pl.BlockSpec((1, D_MODEL, d_ff2), lambda t, te, nu: (te[t], 0, 0)),
                  pl.BlockSpec((1, 1, d_ff2), lambda t, te, nu: (te[t], 0, 0)),
                  pl.BlockSpec((1, d_ff2 // 2, D_MODEL), lambda t, te, nu: (te[t], 0, 0)),
                  pl.BlockSpec((1, 1, D_MODEL), lambda t, te, nu: (te[t], 0, 0))],
        out_specs=pl.BlockSpec((MOE_BLOCK, ROW_SLABS, LANES), lambda t, te, nu: (t, 0, 0)),
        scratch_shapes=[pltpu.VMEM((2, MOE_BLOCK, ROW_SLABS, LANES), F32), pltpu.SemaphoreType.DMA((2,))],
    )
    rows = row_tok.reshape(n_tiles, 1, MOE_BLOCK)
    return pl.pallas_call(
        _expert_kernel,
        grid_spec=grid_spec,
        out_shape=jax.ShapeDtypeStruct((n_tiles * MOE_BLOCK, ROW_SLABS, LANES), F32),
        compiler_params=_params(),
        name="moe_experts",
    )(tile_e, n_used, rows, rows, h, w_gu, b_gu, w_down, b_down)


def _combine_kernel(dest_ref, dest_next_ref, y_hbm, x_ref, gate_ref, g2_ref, o_ref, ybuf, sems, *, tm):
    i = pl.program_id(0)
    slot = i % 2
    n_rows = TOP_K * tm

    @pl.when(i == 0)
    def _():
        _start_row_gather(y_hbm, ybuf.at[0], sems.at[0], dest_ref, n_rows)

    @pl.when(i + 1 < pl.num_programs(0))
    def _():
        _start_row_gather(y_hbm, ybuf.at[1 - slot], sems.at[1 - slot], dest_next_ref, n_rows)

    _wait_row_gather(y_hbm, ybuf.at[slot], sems.at[slot], n_rows)
    gate = gate_ref[...]
    g2 = g2_ref[0]
    for s in range(ROW_SLABS):
        cols = slice(s * LANES, (s + 1) * LANES)
        acc = gate[:, 0:1] * ybuf[slot, 0:tm, s, :]
        for k in range(1, TOP_K):
            acc = acc + gate[:, k:k + 1] * ybuf[slot, k * tm:(k + 1) * tm, s, :]
        o_ref[:, cols] = x_ref[:, cols] + g2[:, cols] * acc


def moe_combine(x, y_rows, dest, gate, g2, segs, tm=256):
    n = x.shape[0]
    nt = n // tm
    dest_tiles = dest.reshape(TOP_K, nt, tm).transpose(1, 0, 2).reshape(nt, 1, TOP_K * tm)
    seg_of = lambda i: (i * tm) // segs.seg
    return pl.pallas_call(
        functools.partial(_combine_kernel, tm=tm),
        grid=(nt,),
        in_specs=[pl.BlockSpec((1, 1, TOP_K * tm), lambda i: (i, 0, 0), memory_space=pltpu.SMEM),
                  pl.BlockSpec((1, 1, TOP_K * tm), lambda i: (jnp.minimum(i + 1, nt - 1), 0, 0),
                               memory_space=pltpu.SMEM),
                  pl.BlockSpec(memory_space=pl.ANY),
                  pl.BlockSpec((tm, D_MODEL), lambda i: (i, 0)),
                  pl.BlockSpec((tm, TOP_K), lambda i: (i, 0)),
                  pl.BlockSpec((1, 1, D_MODEL), lambda i: (seg_of(i), 0, 0))],
        out_specs=pl.BlockSpec((tm, D_MODEL), lambda i: (i, 0)),
        out_shape=jax.ShapeDtypeStruct((n, D_MODEL), F32),
        scratch_shapes=[pltpu.VMEM((2, TOP_K * tm, ROW_SLABS, LANES), F32), pltpu.SemaphoreType.DMA((2,))],
        compiler_params=_params(),
        name="moe_combine",
    )(dest_tiles, dest_tiles, y_rows, x, gate, g2)


def moe_layer(x, g, sc, sh, g2, router_w, router_b, w_gu, b_gu, w_down, b_down, segs):
    n = x.shape[0]
    n_tiles = -(-(n * TOP_K + N_EXPERTS * (MOE_BLOCK - 1)) // MOE_BLOCK)
    h, top_idx_t, gate_t = norm_router(x, g, sc, sh, router_w, router_b, segs)
    dest, row_tok, tile_e, n_used = moe_routing(top_idx_t, n_tiles)
    y_rows = moe_experts(h, row_tok, tile_e, n_used, w_gu, b_gu, w_down, b_down)
    return moe_combine(x, y_rows, dest, gate_t.T, g2, segs)


def _pad_cols(w, width):
    return jnp.pad(w, ((0, 0), (0, width - w.shape[1])))


def trunk(x, c, p, segs):
    nseg = segs.nseg
    rows = -(-nseg // 8) * 8
    c_pad = jnp.pad(c, ((0, rows - nseg), (0, 0)))
    mod = ada_modulation(c_pad, p['ada_w'], p['ada_b'])[:, :nseg].reshape(DEPTH, nseg, 6, 1, D_MODEL)

    for layer in range(DEPTH):
        sh1, sc1, g1, sh2, sc2, g2 = [mod[layer, :, j] for j in range(6)]
        i = layer // 2
        n1 = p['norm1_g'][layer].reshape(1, -1)
        if layer % 2 == 0:
            w_in = p['ev_w_in'][i].astype(BF16)
            w_a = w_in[:, :1536]
            w_lr = _pad_cols(w_in[:, 1536:1568], 128)
            w_c = w_in[:, 1568:]
            a_proj, lr_proj, c_proj = norm_proj(x, n1, sc1, sh1, [w_a, w_lr, w_c], segs)
            o_f, o_b = gla_bidir(a_proj, lr_proj, p['gla_wa_f'][i], p['gla_ba_f'][i],
                                 p['gla_wa_b'][i], p['gla_ba_b'][i], segs)
            x = even_out(x, o_f, o_b, a_proj, c_proj, p['gla_norm_g'][i], p['conv_w'][i],
                         p['ev_w_out'][i].astype(BF16), g1, segs)
        else:
            w_in = p['od_w_in'][i].astype(BF16)
            q_proj, kv_proj, du = norm_proj(x, n1, sc1, sh1, [w_in[:, :512], w_in[:, 512:768], w_in[:, 768:]], segs)
            att = windowed_attention(q_proj, kv_proj, p['q_norm_g'][i], p['k_norm_g'][i], p['attn_sink'][i], segs)
            s5_params = (p['s5_lam_re'][i], p['s5_lam_im'][i], p['s5_log_step'][i],
                         p['s5_b_re'][i], p['s5_b_im'][i], p['s5_c_re'][i], p['s5_c_im'][i])
            ys = s5_conv(du, s5_params, segs)
            x = odd_out(x, att, du, ys, p['s5_d'][i], p['s5_glu_w'][i].astype(BF16), p['s5_glu_b'][i],
                        p['od_w_out'][i].astype(BF16), g1, segs)
        x = moe_layer(x, p['norm2_g'][layer].reshape(1, -1), sc2, sh2, g2,
                      p['router_w'][layer], p['router_b'][layer],
                      p['exp_w_gu'][layer].astype(BF16), p['exp_b_gu'][layer].reshape(N_EXPERTS, 1, -1),
                      p['exp_w_down'][layer].astype(BF16), p['exp_b_down'][layer].reshape(N_EXPERTS, 1, -1), segs)
    return x


def kernel(x_prompt, x_sample, c_prompt, c_sample, ada_w, ada_b, norm1_g, norm2_g, ev_w_in, ev_w_out, gla_wa_f, gla_ba_f, gla_wa_b, gla_ba_b, gla_norm_g, conv_w, od_w_in, od_w_out, q_norm_g, k_norm_g, attn_sink, s5_lam_re, s5_lam_im, s5_log_step, s5_b_re, s5_b_im, s5_c_re, s5_c_im, s5_d, s5_glu_w, s5_glu_b, router_w, router_b, exp_w_gu, exp_b_gu, exp_w_down, exp_b_down):
    p = dict(ada_w=ada_w, ada_b=ada_b, norm1_g=norm1_g, norm2_g=norm2_g,
             ev_w_in=ev_w_in, ev_w_out=ev_w_out, gla_wa_f=gla_wa_f, gla_ba_f=gla_ba_f,
             gla_wa_b=gla_wa_b, gla_ba_b=gla_ba_b, gla_norm_g=gla_norm_g, conv_w=conv_w,
             od_w_in=od_w_in, od_w_out=od_w_out, q_norm_g=q_norm_g, k_norm_g=k_norm_g,
             attn_sink=attn_sink, s5_lam_re=s5_lam_re, s5_lam_im=s5_lam_im, s5_log_step=s5_log_step,
             s5_b_re=s5_b_re, s5_b_im=s5_b_im, s5_c_re=s5_c_re, s5_c_im=s5_c_im, s5_d=s5_d,
             s5_glu_w=s5_glu_w, s5_glu_b=s5_glu_b, router_w=router_w, router_b=router_b,
             exp_w_gu=exp_w_gu, exp_b_gu=exp_b_gu, exp_w_down=exp_w_down, exp_b_down=exp_b_down)
    bp, tp, _ = x_prompt.shape
    bs, ts, _ = x_sample.shape
    seg = math.gcd(tp, ts)
    per_p, per_s = tp // seg, ts // seg
    assert bs == 1 and per_p == 1 and per_s == 2, "segment layout: prompt sequences of one segment, one sample sequence of two"
    nseg = bp * per_p + bs * per_s
    segs = Segs(seg=seg, nseg=nseg, join=bp * per_p + 1)
    x = jnp.concatenate([x_prompt.reshape(-1, D_MODEL), x_sample.reshape(-1, D_MODEL)], axis=0)
    c = jnp.concatenate([c_prompt, jnp.repeat(c_sample, per_s, axis=0)], axis=0)
    y = trunk(x, c, p, segs)
    n_p = bp * tp
    return y[:n_p].reshape(x_prompt.shape), y[n_p:].reshape(x_sample.shape)
```

```python
import functools
import math
from typing import NamedTuple

import numpy as np
import jax
import jax.numpy as jnp
from jax import lax
from jax.experimental import pallas as pl
from jax.experimental.pallas import tpu as pltpu

F32 = jnp.float32
BF16 = jnp.bfloat16

D_MODEL = 1024
DEPTH = 4
HALF = 512
HEAD_DIM = 64

GLA_HEADS = 4
GLA_DV = 128
GLA_DK = 64
GLA_RANK = 16
GLA_TAU = 16.0
GLA_CHUNK = 64

CONV_WIDTH = 3

ATT_Q_HEADS = 8
ATT_KV_HEADS = 2
ATT_WINDOW = 128

S5_GROUP = 16
S5_GROUPS = 32
S5_STATE = 64
S5_CHUNK = 16

N_EXPERTS = 32
TOP_K = 4
SWIGLU_LIMIT = 7.0
SWIGLU_ALPHA = 1.702
MOE_BLOCK = 256

NORM_EPS = 1e-6
NEG_BIG = -1e30

LANES = 128
ROW_SLABS = D_MODEL // LANES
GATHER_UNROLL = 8

VMEM_LIMIT_BYTES = 52 * 1024 * 1024

NT_DIMS = (((1,), (1,)), ((), ()))
TN_DIMS = (((0,), (0,)), ((), ()))


class Segs(NamedTuple):
    seg: int
    nseg: int
    join: int


def _params(n_axes=1):
    return pltpu.CompilerParams(dimension_semantics=("arbitrary",) * n_axes,
                                vmem_limit_bytes=VMEM_LIMIT_BYTES)


def _split_bf16(a):
    hi = a.astype(BF16)
    lo = (a - hi.astype(F32)).astype(BF16)
    return hi, lo


def _dot(a, b):
    return jnp.dot(a, b, preferred_element_type=F32)


def _dot_split(a, b):
    a_hi, a_lo = _split_bf16(a)
    b_hi, b_lo = _split_bf16(b)
    return _dot(a_hi, b_hi) + _dot(a_lo, b_hi) + _dot(a_hi, b_lo)


def _sublane_transpose8(vs):
    axis = vs[0].ndim - 2
    sub = lax.broadcasted_iota(jnp.int32, vs[0].shape, axis)
    vs = list(vs)
    for d in (4, 2, 1):
        low = (sub & d) == 0
        out = list(vs)
        for i in range(8):
            if i & d == 0:
                out[i] = jnp.where(low, vs[i], pltpu.roll(vs[i + d], d, axis=axis))
                out[i + d] = jnp.where(low, pltpu.roll(vs[i], 8 - d, axis=axis), vs[i + d])
        vs = out
    return vs


def _slabs_to_rows(slabs):
    rows = slabs.shape[0]
    groups = slabs.reshape(rows // 8, 8, ROW_SLABS, LANES)
    cols = _sublane_transpose8([groups[:, i] for i in range(8)])
    return jnp.concatenate([c.reshape(rows, LANES) for c in cols], axis=1)


def _rows_to_slabs(value):
    rows = value.shape[0]
    cols = [value[:, s * LANES:(s + 1) * LANES].reshape(rows // 8, 8, LANES) for s in range(ROW_SLABS)]
    tiles = _sublane_transpose8(cols)
    return jnp.stack(tiles, axis=1).reshape(rows, ROW_SLABS, LANES)


def _rms_mod(x, g, sc, sh):
    ms = jnp.mean(x * x, axis=-1, keepdims=True)
    h = x * lax.rsqrt(ms + NORM_EPS) * g
    return h * (1.0 + sc) + sh


def _ada_kernel(c_ref, w_ref, b_ref, o_ref):
    c = c_ref[...]
    s = c * jax.nn.sigmoid(c)
    o_ref[0] = _dot_split(s, w_ref[0]) + b_ref[0]


def ada_modulation(c_pad, ada_w, ada_b):
    rows = c_pad.shape[0]
    cols = 6 * D_MODEL
    tn = 1536
    return pl.pallas_call(
        _ada_kernel,
        grid=(DEPTH, cols // tn),
        in_specs=[pl.BlockSpec((rows, D_MODEL), lambda l, j: (0, 0)),
                  pl.BlockSpec((1, D_MODEL, tn), lambda l, j: (l, 0, j)),
                  pl.BlockSpec((1, 1, tn), lambda l, j: (l, 0, j))],
        out_specs=pl.BlockSpec((1, rows, tn), lambda l, j: (l, 0, j)),
        out_shape=jax.ShapeDtypeStruct((DEPTH, rows, cols), F32),
        compiler_params=_params(2),
        name="ada_modulation",
    )(c_pad, ada_w, ada_b.reshape(DEPTH, 1, cols))


def _norm_proj_kernel(x_ref, g_ref, sc_ref, sh_ref, *refs, n_w):
    h = _rms_mod(x_ref[...], g_ref[...], sc_ref[0], sh_ref[0]).astype(BF16)
    for w_ref, o_ref in zip(refs[:n_w], refs[n_w:]):
        o_ref[...] = _dot(h, w_ref[...]).astype(o_ref.dtype)


def norm_proj(x, g, sc, sh, weights, segs, tm=512):
    n = x.shape[0]
    n_w = len(weights)
    seg_of = lambda i: (i * tm) // segs.seg
    in_specs = [pl.BlockSpec((tm, D_MODEL), lambda i: (i, 0)),
                pl.BlockSpec((1, D_MODEL), lambda i: (0, 0)),
                pl.BlockSpec((1, 1, D_MODEL), lambda i: (seg_of(i), 0, 0)),
                pl.BlockSpec((1, 1, D_MODEL), lambda i: (seg_of(i), 0, 0))]
    in_specs += [pl.BlockSpec(w.shape, lambda i: (0, 0)) for w in weights]
    return pl.pallas_call(
        functools.partial(_norm_proj_kernel, n_w=n_w),
        grid=(n // tm,),
        in_specs=in_specs,
        out_specs=[pl.BlockSpec((tm, w.shape[1]), lambda i: (i, 0)) for w in weights],
        out_shape=[jax.ShapeDtypeStruct((n, w.shape[1]), F32) for w in weights],
        compiler_params=_params(),
        name="norm_proj",
    )(x, g, sc, sh, *weights)


def _gla_kernel(qk_f, v_f, lr_f, qk_b, v_b, lr_b, wa_f, ba_f, wa_b, ba_b, cum_f, cum_b,
                of_ref, ob_ref, st_f, st_b, *, segs, rows):
    i = pl.program_id(0)
    nb = pl.num_programs(0)
    ib = nb - 1 - i
    bps = segs.seg // rows
    seg_f = i // bps
    seg_b = ib // bps

    @pl.when((i % bps == 0) & (seg_f != segs.join))
    def _():
        st_f[...] = jnp.zeros_like(st_f)

    @pl.when((ib % bps == bps - 1) & (seg_b + 1 != segs.join))
    def _():
        st_b[...] = jnp.zeros_like(st_b)

    L = GLA_CHUNK
    scale = GLA_DK ** -0.5
    n_chunks = rows // L
    dkw = GLA_HEADS * GLA_DK

    def direction(qk_ref, v_ref, lr_ref, wa_ref, ba_ref, cum_ref, st_ref, o_ref, reverse):
        qk = qk_ref[...]
        q = qk[:, :dkw] * scale
        k = qk[:, dkw:]
        v = v_ref[...].astype(BF16)
        pre = _dot_split(lr_ref[...], wa_ref[...]) + ba_ref[...]
        log_sig = jnp.minimum(pre, 0.0) - jnp.log(1.0 + jnp.exp(-jnp.abs(pre)))
        g_hi, g_lo = _split_bf16(log_sig * (1.0 / GLA_TAU))
        cum = cum_ref[...]
        b = _dot(cum, g_hi) + _dot(cum, g_lo)
        edge = 0 if reverse else L - 1
        b_edge = jnp.concatenate([jnp.broadcast_to(b[c * L + edge:c * L + edge + 1, :], (L, dkw))
                                  for c in range(n_chunks)], axis=0)
        qe = (q * jnp.exp(b)).astype(BF16)
        ke = (k * jnp.exp(-b)).astype(BF16)
        kd = (k * jnp.exp(b_edge - b)).astype(BF16)
        d_edge = jnp.exp(b_edge)
        mask = cum != 0
        order = range(n_chunks - 1, -1, -1) if reverse else range(n_chunks)
        outs = []
        for h in range(GLA_HEADS):
            ks = slice(h * GLA_DK, (h + 1) * GLA_DK)
            qeh, keh, kdh = qe[:, ks], ke[:, ks], kd[:, ks]
            vh = v[:, h * GLA_DV:(h + 1) * GLA_DV]
            att = lax.dot_general(qeh, keh, NT_DIMS, preferred_element_type=F32)
            o = _dot(jnp.where(mask, att, 0.0).astype(BF16), vh)
            st = st_ref[h]
            inter = [None] * n_chunks
            for c in order:
                rs = slice(c * L, (c + 1) * L)
                inter[c] = lax.dot_general(qeh[rs], st.astype(BF16), NT_DIMS, preferred_element_type=F32)
                st = st * d_edge[c * L:c * L + 1, ks] + lax.dot_general(vh[rs], kdh[rs], TN_DIMS,
                                                                        preferred_element_type=F32)
            st_ref[h] = st
            outs.append(o + jnp.concatenate(inter, axis=0))
        o_ref[...] = jnp.concatenate(outs, axis=1)

    direction(qk_f, v_f, lr_f, wa_f, ba_f, cum_f, st_f, of_ref, False)
    direction(qk_b, v_b, lr_b, wa_b, ba_b, cum_b, st_b, ob_ref, True)


def gla_bidir(a_proj, lr_proj, wa_f, ba_f, wa_b, ba_b, segs, rows=512):
    n = a_proj.shape[0]
    nb = n // rows
    lanes = lr_proj.shape[1]
    wa_f_pad = jnp.zeros((lanes, GLA_HEADS * GLA_DK), F32).at[:GLA_RANK].set(wa_f)
    wa_b_pad = jnp.zeros((lanes, GLA_HEADS * GLA_DK), F32).at[GLA_RANK:2 * GLA_RANK].set(wa_b)
    t_io = jnp.arange(rows, dtype=jnp.int32)
    same_chunk = (t_io[:, None] // GLA_CHUNK) == (t_io[None, :] // GLA_CHUNK)
    cum_f = (same_chunk & (t_io[None, :] <= t_io[:, None])).astype(BF16)
    cum_b = (same_chunk & (t_io[None, :] >= t_io[:, None])).astype(BF16)
    fwd = lambda i: (i, 0)
    bwd = lambda i: (nb - 1 - i, 0)
    const = lambda i: (0, 0)
    spec = lambda width, col, row_map: pl.BlockSpec((rows, width), lambda i: (row_map(i)[0], col))
    return pl.pallas_call(
        functools.partial(_gla_kernel, segs=segs, rows=rows),
        grid=(nb,),
        in_specs=[spec(512, 0, fwd), spec(512, 1, fwd), pl.BlockSpec((rows, lanes), fwd),
                  spec(512, 0, bwd), spec(512, 1, bwd), pl.BlockSpec((rows, lanes), bwd),
                  pl.BlockSpec(wa_f_pad.shape, const), pl.BlockSpec((1, 256), const),
                  pl.BlockSpec(wa_b_pad.shape, const), pl.BlockSpec((1, 256), const),
                  pl.BlockSpec((rows, rows), const), pl.BlockSpec((rows, rows), const)],
        out_specs=[pl.BlockSpec((rows, HALF), fwd), pl.BlockSpec((rows, HALF), bwd)],
        out_shape=[jax.ShapeDtypeStruct((n, HALF), F32)] * 2,
        scratch_shapes=[pltpu.VMEM((GLA_HEADS, GLA_DV, GLA_DK), F32)] * 2,
        compiler_params=_params(),
        name="gla_bidir",
    )(a_proj, a_proj, lr_proj, a_proj, a_proj, lr_proj,
      wa_f_pad, ba_f.reshape(1, -1), wa_b_pad, ba_b.reshape(1, -1), cum_f, cum_b)


def _even_out_kernel(x_ref, of_ref, ob_ref, og_ref, c_ref, cprev_ref, cnext_ref, ng_ref, cw_ref,
                     w_ref, g1_ref, o_ref, *, segs, tm):
    i = pl.program_id(0)
    tps = segs.seg // tm
    seg = i // tps
    has_prev = jnp.logical_not((i % tps == 0) & (seg != segs.join))
    has_next = jnp.logical_not((i % tps == tps - 1) & (seg + 1 != segs.join))

    o = of_ref[...] + ob_ref[...]
    og = og_ref[...]
    parts = []
    for h in range(GLA_HEADS):
        oh = o[:, h * GLA_DV:(h + 1) * GLA_DV]
        ms = jnp.mean(oh * oh, axis=-1, keepdims=True)
        parts.append(oh * lax.rsqrt(ms + NORM_EPS) * ng_ref[...])
    a_out = jnp.concatenate(parts, axis=1) * (og * jax.nn.sigmoid(og))

    c = c_ref[...]
    bg = c[:, :HALF]
    u = c[:, HALF:2 * HALF] * c[:, 2 * HALF:]
    cp = cprev_ref[7:8, :]
    cn = cnext_ref[0:1, :]
    u_prev_edge = jnp.where(has_prev, cp[:, HALF:2 * HALF] * cp[:, 2 * HALF:], 0.0)
    u_next_edge = jnp.where(has_next, cn[:, HALF:2 * HALF] * cn[:, 2 * HALF:], 0.0)
    row = lax.broadcasted_iota(jnp.int32, (tm, 1), 0)
    u_prev = jnp.where(row == 0, u_prev_edge, pltpu.roll(u, 1, axis=0))
    u_next = jnp.where(row == tm - 1, u_next_edge, pltpu.roll(u, tm - 1, axis=0))
    cw = cw_ref[...]
    b_out = bg * (cw[0:1, :] * u_prev + cw[1:2, :] * u + cw[2:3, :] * u_next)

    w = w_ref[...]
    mix = _dot(a_out.astype(BF16), w[:HALF, :]) + _dot(b_out.astype(BF16), w[HALF:, :])
    o_ref[...] = x_ref[...] + g1_ref[0] * mix


def even_out(x, o_f, o_b, a_proj, c_proj, norm_g, conv_w, w_out, g1, segs, tm=512):
    n = x.shape[0]
    last8 = n // 8 - 1
    seg_of = lambda i: (i * tm) // segs.seg
    return pl.pallas_call(
        functools.partial(_even_out_kernel, segs=segs, tm=tm),
        grid=(n // tm,),
        in_specs=[pl.BlockSpec((tm, D_MODEL), lambda i: (i, 0)),
                  pl.BlockSpec((tm, HALF), lambda i: (i, 0)),
                  pl.BlockSpec((tm, HALF), lambda i: (i, 0)),
                  pl.BlockSpec((tm, HALF), lambda i: (i, 2)),
                  pl.BlockSpec((tm, 3 * HALF), lambda i: (i, 0)),
                  pl.BlockSpec((8, 3 * HALF), lambda i: (jnp.maximum(i * (tm // 8) - 1, 0), 0)),
                  pl.BlockSpec((8, 3 * HALF), lambda i: (jnp.minimum((i + 1) * (tm // 8), last8), 0)),
                  pl.BlockSpec((1, GLA_DV), lambda i: (0, 0)),
                  pl.BlockSpec((CONV_WIDTH, HALF), lambda i: (0, 0)),
                  pl.BlockSpec((D_MODEL, D_MODEL), lambda i: (0, 0)),
                  pl.BlockSpec((1, 1, D_MODEL), lambda i: (seg_of(i), 0, 0))],
        out_specs=pl.BlockSpec((tm, D_MODEL), lambda i: (i, 0)),
        out_shape=jax.ShapeDtypeStruct((n, D_MODEL), F32),
        compiler_params=_params(),
        name="even_out",
    )(x, o_f, o_b, a_proj, c_proj, c_proj, c_proj, norm_g.reshape(1, -1), conv_w, w_out, g1)


def _attn_kernel(q_ref, kv_ref, kvp_ref, kvn_ref, qg_ref, kg_ref, sink_ref, o_ref, *, segs, tq):
    i = pl.program_id(0)
    W = ATT_WINDOW
    tps = segs.seg // tq
    seg = i // tps
    first = (i % tps == 0) & (seg != segs.join)
    last = (i % tps == tps - 1) & (seg + 1 != segs.join)

    kv_all = jnp.concatenate([kvp_ref[...], kv_ref[...], kvn_ref[...]], axis=0)
    kvw = ATT_KV_HEADS * HEAD_DIM
    k_heads, v_heads = [], []
    for h in range(ATT_KV_HEADS):
        kh = kv_all[:, h * HEAD_DIM:(h + 1) * HEAD_DIM]
        ms = jnp.mean(kh * kh, axis=-1, keepdims=True)
        k_heads.append((kh * lax.rsqrt(ms + NORM_EPS) * kg_ref[...]).astype(BF16))
        v_heads.append(kv_all[:, kvw + h * HEAD_DIM:kvw + (h + 1) * HEAD_DIM].astype(BF16))

    t_io = lax.broadcasted_iota(jnp.int32, (W, 3 * W), 0)
    j_io = lax.broadcasted_iota(jnp.int32, (W, 3 * W), 1)
    rel = j_io - W - t_io
    dist = jnp.abs(rel)
    in_window = dist <= W
    dist_f = dist.astype(F32)
    group = ATT_Q_HEADS // ATT_KV_HEADS
    n_blk = tq // W
    for blk in range(n_blk):
        valid = in_window
        if blk == 0:
            valid = valid & ((j_io >= W) | jnp.logical_not(first))
        if blk == n_blk - 1:
            valid = valid & ((j_io < 2 * W) | jnp.logical_not(last))
        q = q_ref[blk * W:(blk + 1) * W, :]
        outs = []
        for hq in range(ATT_Q_HEADS):
            kvh = hq // group
            qh = q[:, hq * HEAD_DIM:(hq + 1) * HEAD_DIM]
            ms = jnp.mean(qh * qh, axis=-1, keepdims=True)
            qn = (qh * lax.rsqrt(ms + NORM_EPS) * qg_ref[...] * (HEAD_DIM ** -0.5)).astype(BF16)
            kh = k_heads[kvh][blk * W:(blk + 3) * W, :]
            vh = v_heads[kvh][blk * W:(blk + 3) * W, :]
            s = lax.dot_general(qn, kh, NT_DIMS, preferred_element_type=F32)
            slope = 2.0 ** (-8.0 * (hq + 1) / ATT_Q_HEADS)
            s = jnp.where(valid, s - slope * dist_f, NEG_BIG)
            sk = sink_ref[hq]
            m = jnp.maximum(jnp.max(s, axis=-1, keepdims=True), sk)
            p = jnp.exp(s - m)
            denom = jnp.sum(p, axis=-1, keepdims=True) + jnp.exp(sk - m)
            outs.append(_dot(p.astype(BF16), vh) / denom)
        o_ref[blk * W:(blk + 1) * W, :] = jnp.concatenate(outs, axis=1)


def windowed_attention(q_proj, kv_proj, q_norm_g, k_norm_g, sink, segs, tq=512):
    n = q_proj.shape[0]
    W = ATT_WINDOW
    r = tq // W
    last = n // W - 1
    kvw = 2 * ATT_KV_HEADS * HEAD_DIM
    return pl.pallas_call(
        functools.partial(_attn_kernel, segs=segs, tq=tq),
        grid=(n // tq,),
        in_specs=[pl.BlockSpec((tq, HALF), lambda i: (i, 0)),
                  pl.BlockSpec((tq, kvw), lambda i: (i, 0)),
                  pl.BlockSpec((W, kvw), lambda i: (jnp.maximum(i * r - 1, 0), 0)),
                  pl.BlockSpec((W, kvw), lambda i: (jnp.minimum((i + 1) * r, last), 0)),
                  pl.BlockSpec((1, HEAD_DIM), lambda i: (0, 0)),
                  pl.BlockSpec((1, HEAD_DIM), lambda i: (0, 0)),
                  pl.BlockSpec(memory_space=pltpu.SMEM)],
        out_specs=pl.BlockSpec((tq, HALF), lambda i: (i, 0)),
        out_shape=jax.ShapeDtypeStruct((n, HALF), F32),
        compiler_params=_params(),
        name="windowed_attention",
    )(q_proj, kv_proj, kv_proj, kv_proj, q_norm_g.reshape(1, -1), k_norm_g.reshape(1, -1), sink)


def s5_tables(lam_re, lam_im, log_step, b_re, b_im, c_re, c_im, *, n_inner):
    L = S5_CHUNK
    dt = jnp.exp(log_step)[:, :, None]
    lr, li = lam_re, lam_im
    mag = jnp.exp(lr * dt)
    ar, ai = mag * jnp.cos(li * dt), mag * jnp.sin(li * dt)
    den = lr * lr + li * li
    zr = ((ar - 1.0) * lr + ai * li) / den
    zi = (ai * lr - (ar - 1.0) * li) / den
    bbr = zr[..., None] * b_re - zi[..., None] * b_im
    bbi = zr[..., None] * b_im + zi[..., None] * b_re
    tau = jnp.arange(L + 1, dtype=F32)[:, None, None, None]
    pmag = jnp.exp(lr[None] * dt[None] * tau)
    pang = li[None] * dt[None] * tau
    pr, pi = pmag * jnp.cos(pang), pmag * jnp.sin(pang)

    hp = lax.Precision.HIGHEST
    car = c_re[None] * pr[:, :, :, None, :] - c_im[None] * pi[:, :, :, None, :]
    cai = c_re[None] * pi[:, :, :, None, :] + c_im[None] * pr[:, :, :, None, :]
    kern = (jnp.einsum('ldgop,dgpi->ldgoi', car, bbr, precision=hp)
            - jnp.einsum('ldgop,dgpi->ldgoi', cai, bbi, precision=hp))
    s_idx = jnp.arange(L)[:, None]
    t_idx = jnp.arange(L)[None, :]
    lag_f = jnp.clip(t_idx - s_idx, 0, L)
    lag_b = jnp.clip(s_idx - t_idx, 0, L)
    kf = jnp.where((t_idx >= s_idx)[:, :, None, None, None], kern[:, 0][lag_f], 0.0)
    kb = jnp.where((s_idx >= t_idx)[:, :, None, None, None], kern[:, 1][lag_b], 0.0)
    m = (kf + kb).transpose(2, 0, 4, 1, 3).reshape(S5_GROUPS, L * S5_GROUP, L * S5_GROUP)

    abr = pr[..., None] * bbr[None] - pi[..., None] * bbi[None]
    abi = pr[..., None] * bbi[None] + pi[..., None] * bbr[None]
    e_f = L - 1 - jnp.arange(L)
    e_b = jnp.arange(L)
    to_rows = lambda a: a.transpose(1, 0, 3, 2).reshape(S5_GROUPS, L * S5_GROUP, S5_STATE)
    wp = jnp.concatenate([to_rows(abr[e_f, 0]), to_rows(abi[e_f, 0]),
                          to_rows(abr[e_b, 1]), to_rows(abi[e_b, 1])], axis=-1)

    o_f = jnp.arange(L) + 1
    o_b = L - jnp.arange(L)
    to_cols = lambda a: a.transpose(1, 3, 0, 2).reshape(S5_GROUPS, S5_STATE, L * S5_GROUP)
    wc = jnp.concatenate([to_cols(car[o_f, 0]), -to_cols(cai[o_f, 0]),
                          to_cols(car[o_b, 1]), -to_cols(cai[o_b, 1])], axis=1)

    steps = jnp.arange(n_inner + 1, dtype=F32)[:, None, None, None] * float(L)
    qmag = jnp.exp(lr[None] * dt[None] * steps)
    qang = li[None] * dt[None] * steps
    qr, qi = qmag * jnp.cos(qang), qmag * jnp.sin(qang)
    form_a = jnp.concatenate([qr, qr], -1)
    form_b = jnp.concatenate([-qi, qi], -1)
    dpow = jnp.stack([form_a[:, 0], form_b[:, 0], form_a[:, 1], form_b[:, 1]], axis=0)
    dpow = dpow.transpose(2, 0, 1, 3)
    return m.astype(BF16), wp.astype(BF16), wc, dpow


def _s5_kernel(x_ref, m_ref, wp_ref, wc_ref, dpow_ref, y_ref, pf_scr, pb_scr, sf_scr, sb_scr, *, segs, n_sub):
    P2 = 2 * S5_STATE
    n_inner = segs.seg // S5_CHUNK // n_sub
    rows_step = segs.nseg * n_sub
    x = x_ref[0]
    y_ref[0] = _dot(x, m_ref[0])
    p = _dot(x, wp_ref[0])
    pf_scr[...] = p[:, :P2]
    pb_scr[...] = p[:, P2:]

    def cmul(s, s_swapped, form_a, form_b):
        return s * form_a + s_swapped * form_b

    def swap(s):
        return pltpu.roll(s, S5_STATE, axis=1)

    def power(form, i):
        return dpow_ref[0, form, pl.ds(i, 1), :]

    d1 = [power(f, 1) for f in range(4)]
    dn = [power(f, n_inner) for f in range(4)]

    def block(i):
        return pl.ds(pl.multiple_of(i * rows_step, 8), rows_step)

    def local_step(i, carry):
        s_f, s_b = carry
        rf, rb = block(i), block(n_inner - 1 - i)
        sf_scr[rf, :] = s_f
        sb_scr[rb, :] = s_b
        return (cmul(s_f, swap(s_f), d1[0], d1[1]) + pf_scr[rf, :],
                cmul(s_b, swap(s_b), d1[2], d1[3]) + pb_scr[rb, :])
    zeros = jnp.zeros((rows_step, P2), F32)
    end_f, end_b = lax.fori_loop(0, n_inner, local_step, (zeros, zeros))

    r_io = lax.broadcasted_iota(jnp.int32, (rows_step, 1), 0)
    j_io, seg_io = r_io % n_sub, r_io // n_sub
    takes_prev = jnp.logical_not((j_io == 0) & (seg_io != segs.join))
    takes_next = jnp.logical_not((j_io == n_sub - 1) & (seg_io + 1 != segs.join))
    end_f_prev = pltpu.roll(end_f, 1, axis=0)
    end_b_next = pltpu.roll(end_b, rows_step - 1, axis=0)
    c_f, c_b = zeros, zeros
    longest = n_sub * (2 if segs.join >= 0 else 1)
    for _ in range(longest - 1):
        c_prev = pltpu.roll(c_f, 1, axis=0)
        c_f = jnp.where(takes_prev, cmul(c_prev, swap(c_prev), dn[0], dn[1]) + end_f_prev, 0.0)
        c_next = pltpu.roll(c_b, rows_step - 1, axis=0)
        c_b = jnp.where(takes_next, cmul(c_next, swap(c_next), dn[2], dn[3]) + end_b_next, 0.0)

    c_f_sw, c_b_sw = swap(c_f), swap(c_b)

    def fix_step(i, carry):
        rf, rb = block(i), block(n_inner - 1 - i)
        sf_scr[rf, :] += cmul(c_f, c_f_sw, power(0, i), power(1, i))
        sb_scr[rb, :] += cmul(c_b, c_b_sw, power(2, i), power(3, i))
        return carry
    lax.fori_loop(0, n_inner, fix_step, 0)

    wc_hi, wc_lo = _split_bf16(wc_ref[0])
    acc = y_ref[0]
    for s_scr, rows in ((sf_scr, slice(0, P2)), (sb_scr, slice(P2, 2 * P2))):
        s_hi, s_lo = _split_bf16(s_scr[...])
        acc = acc + _dot(s_hi, wc_hi[rows]) + _dot(s_lo, wc_hi[rows]) + _dot(s_hi, wc_lo[rows])
    y_ref[0] = acc


def s5_conv(du, params, segs, n_sub=8):
    n = du.shape[0]
    L = S5_CHUNK
    rows = n // L
    width = L * S5_GROUP
    n_inner = segs.seg // L // n_sub
    m, wp, wc, dpow = s5_tables(*params, n_inner=n_inner)
    xg = du.astype(BF16).reshape(segs.nseg, n_sub, n_inner, L, S5_GROUPS, S5_GROUP)
    xg = xg.transpose(4, 2, 0, 1, 3, 5).reshape(S5_GROUPS, rows, width)
    grp = lambda g: (g, 0, 0)
    yg = pl.pallas_call(
        functools.partial(_s5_kernel, segs=segs, n_sub=n_sub),
        grid=(S5_GROUPS,),
        in_specs=[pl.BlockSpec((1, rows, width), grp),
                  pl.BlockSpec((1, width, width), grp),
                  pl.BlockSpec((1, width, 4 * S5_STATE), grp),
                  pl.BlockSpec((1, 4 * S5_STATE, width), grp),
                  pl.BlockSpec((1, 4, n_inner + 1, 2 * S5_STATE), lambda g: (g, 0, 0, 0))],
        out_specs=pl.BlockSpec((1, rows, width), grp),
        out_shape=jax.ShapeDtypeStruct((S5_GROUPS, rows, width), F32),
        scratch_shapes=[pltpu.VMEM((rows, 2 * S5_STATE), F32)] * 4,
        compiler_params=_params(),
        name="s5_conv",
    )(xg, m, wp, wc, dpow)
    yg = yg.reshape(S5_GROUPS, n_inner, segs.nseg, n_sub, L, S5_GROUP)
    return yg.transpose(2, 3, 1, 4, 0, 5).reshape(n, HALF)


def _odd_out_kernel(x_ref, att_ref, du_ref, ys_ref, dsk_ref, gw_ref, gb_ref, w_ref, g1_ref, o_ref):
    y = dsk_ref[...] * du_ref[...] + ys_ref[...]
    z = 0.5 * y * (1.0 + jnp.tanh(math.sqrt(2.0 / math.pi) * (y + 0.044715 * (y * y * y))))
    gate = jax.nn.sigmoid(_dot(z.astype(BF16), gw_ref[...]) + gb_ref[...])
    d_out = z * gate
    w = w_ref[...]
    mix = _dot(att_ref[...].astype(BF16), w[:HALF, :]) + _dot(d_out.astype(BF16), w[HALF:, :])
    o_ref[...] = x_ref[...] + g1_ref[0] * mix


def odd_out(x, att, du, ys, d_skip, glu_w, glu_b, w_out, g1, segs, tm=512):
    n = x.shape[0]
    seg_of = lambda i: (i * tm) // segs.seg
    row = lambda i: (i, 0)
    const = lambda i: (0, 0)
    return pl.pallas_call(
        _odd_out_kernel,
        grid=(n // tm,),
        in_specs=[pl.BlockSpec((tm, D_MODEL), row), pl.BlockSpec((tm, HALF), row),
                  pl.BlockSpec((tm, HALF), row), pl.BlockSpec((tm, HALF), row),
                  pl.BlockSpec((1, HALF), const), pl.BlockSpec((HALF, HALF), const),
                  pl.BlockSpec((1, HALF), const), pl.BlockSpec((D_MODEL, D_MODEL), const),
                  pl.BlockSpec((1, 1, D_MODEL), lambda i: (seg_of(i), 0, 0))],
        out_specs=pl.BlockSpec((tm, D_MODEL), row),
        out_shape=jax.ShapeDtypeStruct((n, D_MODEL), F32),
        compiler_params=_params(),
        name="odd_out",
    )(x, att, du, ys, d_skip.reshape(1, -1), glu_w, glu_b.reshape(1, -1), w_out, g1)


def _norm_router_kernel(x_ref, g_ref, sc_ref, sh_ref, wr_ref, br_ref, before_ref,
                        h_ref, idx_ref, gate_ref, rank_ref, count_ref):
    @pl.when(pl.program_id(0) == 0)
    def _():
        count_ref[...] = jnp.zeros_like(count_ref)

    h = _rms_mod(x_ref[...], g_ref[...], sc_ref[0], sh_ref[0])
    h_ref[...] = _rows_to_slabs(h)
    h_hi, h_lo = _split_bf16(h)
    w_hi, w_lo = _split_bf16(wr_ref[...])
    nt = lambda a, b: lax.dot_general(a, b, NT_DIMS, preferred_element_type=F32)
    logits = nt(w_hi, h_hi) + nt(w_lo, h_hi) + nt(w_hi, h_lo) + br_ref[...]
    e_io = lax.broadcasted_iota(jnp.int32, logits.shape, 0)
    tops, picks = [], []
    for k in range(TOP_K):
        m = jnp.max(logits, axis=0, keepdims=True)
        idx = jnp.min(jnp.where(logits == m, e_io, N_EXPERTS), axis=0, keepdims=True)
        idx_ref[k:k + 1, :] = idx
        pick = e_io == idx
        logits = jnp.where(pick, -jnp.inf, logits)
        tops.append(m)
        picks.append(pick)
    es = [jnp.exp(t - tops[0]) for t in tops]
    total = es[0] + es[1] + es[2] + es[3]
    for k in range(TOP_K):
        gate_ref[k:k + 1, :] = es[k] / total

    chosen = (picks[0] | picks[1] | picks[2] | picks[3]).astype(F32)
    rank = _dot(chosen.astype(BF16), before_ref[...]) + count_ref[:, 0:1]
    for k in range(TOP_K):
        rank_ref[k:k + 1, :] = jnp.sum(jnp.where(picks[k], rank, 0.0), axis=0, keepdims=True).astype(jnp.int32)
    count_ref[...] = count_ref[...] + jnp.sum(chosen, axis=1, keepdims=True)


def norm_router(x, g, sc, sh, router_w, router_b, segs, tm=512):
    n = x.shape[0]
    seg_of = lambda i: (i * tm) // segs.seg
    t_io = jnp.arange(tm, dtype=jnp.int32)
    before = (t_io[:, None] < t_io[None, :]).astype(BF16)
    return pl.pallas_call(
        _norm_router_kernel,
        grid=(n // tm,),
        in_specs=[pl.BlockSpec((tm, D_MODEL), lambda i: (i, 0)),
                  pl.BlockSpec((1, D_MODEL), lambda i: (0, 0)),
                  pl.BlockSpec((1, 1, D_MODEL), lambda i: (seg_of(i), 0, 0)),
                  pl.BlockSpec((1, 1, D_MODEL), lambda i: (seg_of(i), 0, 0)),
                  pl.BlockSpec((N_EXPERTS, D_MODEL), lambda i: (0, 0)),
                  pl.BlockSpec((N_EXPERTS, 1), lambda i: (0, 0)),
                  pl.BlockSpec((tm, tm), lambda i: (0, 0))],
        out_specs=[pl.BlockSpec((tm, ROW_SLABS, LANES), lambda i: (i, 0, 0)),
                   pl.BlockSpec((TOP_K, tm), lambda i: (0, i)),
                   pl.BlockSpec((TOP_K, tm), lambda i: (0, i)),
                   pl.BlockSpec((TOP_K, tm), lambda i: (0, i)),
                   pl.BlockSpec((N_EXPERTS, LANES), lambda i: (0, 0))],
        out_shape=[jax.ShapeDtypeStruct((n, ROW_SLABS, LANES), F32),
                   jax.ShapeDtypeStruct((TOP_K, n), jnp.int32),
                   jax.ShapeDtypeStruct((TOP_K, n), F32),
                   jax.ShapeDtypeStruct((TOP_K, n), jnp.int32),
                   jax.ShapeDtypeStruct((N_EXPERTS, LANES), F32)],
        compiler_params=_params(),
        name="norm_router",
    )(x, g, sc, sh, router_w.T, router_b.reshape(-1, 1), before)


def moe_routing(top_idx_t, rank_t, counts, n_tiles):
    n = top_idx_t.shape[1]
    padded = (counts + MOE_BLOCK - 1) // MOE_BLOCK * MOE_BLOCK
    padded_end = jnp.cumsum(padded)
    padded_start = padded_end - padded
    dest = padded_start[top_idx_t] + rank_t
    n_rows = n_tiles * MOE_BLOCK
    tok = jnp.broadcast_to(jnp.arange(n, dtype=jnp.int32)[None], dest.shape)
    row_tok = jnp.zeros((n_rows,), jnp.int32).at[dest.reshape(-1)].set(tok.reshape(-1), unique_indices=True)
    n_used = (padded_end[-1] // MOE_BLOCK).astype(jnp.int32)
    tile_start = jnp.arange(n_tiles, dtype=jnp.int32) * MOE_BLOCK
    tile_e = jnp.minimum(jnp.searchsorted(padded_end, tile_start, side='right'), N_EXPERTS - 1).astype(jnp.int32)
    last_e = tile_e[jnp.maximum(n_used - 1, 0)]
    tile_e = jnp.where(jnp.arange(n_tiles) < n_used, tile_e, last_e)
    return dest, row_tok, tile_e, n_used.reshape(1)


def _start_row_gather(src_hbm, dst_vmem, sem, idx_ref, n_rows):
    def issue(r, carry):
        pltpu.make_async_copy(src_hbm.at[pl.ds(idx_ref[0, 0, r], 1)], dst_vmem.at[pl.ds(r, 1)], sem).start()
        return carry
    lax.fori_loop(0, n_rows, issue, 0, unroll=GATHER_UNROLL)


def _wait_row_gather(src_hbm, dst_vmem, sem, n_rows):
    pltpu.make_async_copy(src_hbm.at[pl.ds(0, n_rows)], dst_vmem, sem).wait()


def _expert_kernel(tile_e_ref, n_used_ref, rows_ref, rows_next_ref, h_hbm, wgu_ref, bgu_ref, wd_ref, bd_ref,
                   y_ref, xbuf, sems):
    t = pl.program_id(0)
    n_used = n_used_ref[0]
    slot = t % 2

    @pl.when((t == 0) & (n_used > 0))
    def _():
        _start_row_gather(h_hbm, xbuf.at[0], sems.at[0], rows_ref, MOE_BLOCK)

    @pl.when(t + 1 < n_used)
    def _():
        _start_row_gather(h_hbm, xbuf.at[1 - slot], sems.at[1 - slot], rows_next_ref, MOE_BLOCK)

    @pl.when(t < n_used)
    def _():
        _wait_row_gather(h_hbm, xbuf.at[slot], sems.at[slot], MOE_BLOCK)
        x = _slabs_to_rows(xbuf[slot]).astype(BF16)
        gu = _dot(x, wgu_ref[0]) + bgu_ref[0]
        d_ff = gu.shape[1] // 2
        g_ = jnp.minimum(gu[:, :d_ff], SWIGLU_LIMIT)
        u_ = jnp.clip(gu[:, d_ff:], -SWIGLU_LIMIT, SWIGLU_LIMIT)
        hh = g_ * jax.nn.sigmoid(SWIGLU_ALPHA * g_) * (u_ + 1.0)
        y_ref[...] = _rows_to_slabs(_dot(hh.astype(BF16), wd_ref[0]) + bd_ref[0])

    @pl.when(t >= n_used)
    def _():
        y_ref[...] = jnp.zeros_like(y_ref)


def moe_experts(h, row_tok, tile_e, n_used, w_gu, b_gu, w_down, b_down):
    n_tiles = tile_e.shape[0]
    d_ff2 = w_gu.shape[2]
    grid_spec = pltpu.PrefetchScalarGridSpec(
        num_scalar_prefetch=2,
        grid=(n_tiles,),
        in_specs=[pl.BlockSpec((1, 1, MOE_BLOCK), lambda t, te, nu: (t, 0, 0), memory_space=pltpu.SMEM),
                  pl.BlockSpec((1, 1, MOE_BLOCK), lambda t, te, nu: (jnp.minimum(t + 1, n_tiles - 1), 0, 0),
                               memory_space=pltpu.SMEM),
                  pl.BlockSpec(memory_space=pl.ANY),
                  pl.BlockSpec((1, D_MODEL, d_ff2), lambda t, te, nu: (te[t], 0, 0)),
                  pl.BlockSpec((1, 1, d_ff2), lambda t, te, nu: (te[t], 0, 0)),
                  pl.BlockSpec((1, d_ff2 // 2, D_MODEL), lambda t, te, nu: (te[t], 0, 0)),
                  pl.BlockSpec((1, 1, D_MODEL), lambda t, te, nu: (te[t], 0, 0))],
        out_specs=pl.BlockSpec((MOE_BLOCK, ROW_SLABS, LANES), lambda t, te, nu: (t, 0, 0)),
        scratch_shapes=[pltpu.VMEM((2, MOE_BLOCK, ROW_SLABS, LANES), F32), pltpu.SemaphoreType.DMA((2,))],
    )
    rows = row_tok.reshape(n_tiles, 1, MOE_BLOCK)
    return pl.pallas_call(
        _expert_kernel,
        grid_spec=grid_spec,
        out_shape=jax.ShapeDtypeStruct((n_tiles * MOE_BLOCK, ROW_SLABS, LANES), F32),
        compiler_params=_params(),
        name="moe_experts",
    )(tile_e, n_used, rows, rows, h, w_gu, b_gu, w_down, b_down)


def _combine_kernel(dest_ref, dest_next_ref, y_hbm, x_ref, gate_ref, g2_ref, o_ref, ybuf, sems, *, tm):
    i = pl.program_id(0)
    slot = i % 2
    n_rows = TOP_K * tm

    @pl.when(i == 0)
    def _():
        _start_row_gather(y_hbm, ybuf.at[0], sems.at[0], dest_ref, n_rows)

    @pl.when(i + 1 < pl.num_programs(0))
    def _():
        _start_row_gather(y_hbm, ybuf.at[1 - slot], sems.at[1 - slot], dest_next_ref, n_rows)

    _wait_row_gather(y_hbm, ybuf.at[slot], sems.at[slot], n_rows)
    gate = gate_ref[...]
    acc = gate[:, 0:1] * _slabs_to_rows(ybuf[slot, 0:tm])
    for k in range(1, TOP_K):
        acc = acc + gate[:, k:k + 1] * _slabs_to_rows(ybuf[slot, k * tm:(k + 1) * tm])
    o_ref[...] = x_ref[...] + g2_ref[0] * acc


def moe_combine(x, y_rows, dest, gate, g2, segs, tm=256):
    n = x.shape[0]
    nt = n // tm
    dest_tiles = dest.reshape(TOP_K, nt, tm).transpose(1, 0, 2).reshape(nt, 1, TOP_K * tm)
    seg_of = lambda i: (i * tm) // segs.seg
    return pl.pallas_call(
        functools.partial(_combine_kernel, tm=tm),
        grid=(nt,),
        in_specs=[pl.BlockSpec((1, 1, TOP_K * tm), lambda i: (i, 0, 0), memory_space=pltpu.SMEM),
                  pl.BlockSpec((1, 1, TOP_K * tm), lambda i: (jnp.minimum(i + 1, nt - 1), 0, 0),
                               memory_space=pltpu.SMEM),
                  pl.BlockSpec(memory_space=pl.ANY),
                  pl.BlockSpec((tm, D_MODEL), lambda i: (i, 0)),
                  pl.BlockSpec((tm, TOP_K), lambda i: (i, 0)),
                  pl.BlockSpec((1, 1, D_MODEL), lambda i: (seg_of(i), 0, 0))],
        out_specs=pl.BlockSpec((tm, D_MODEL), lambda i: (i, 0)),
        out_shape=jax.ShapeDtypeStruct((n, D_MODEL), F32),
        scratch_shapes=[pltpu.VMEM((2, TOP_K * tm, ROW_SLABS, LANES), F32), pltpu.SemaphoreType.DMA((2,))],
        compiler_params=_params(),
        name="moe_combine",
    )(dest_tiles, dest_tiles, y_rows, x, gate, g2)


def moe_layer(x, g, sc, sh, g2, router_w, router_b, w_gu, b_gu, w_down, b_down, segs):
    n = x.shape[0]
    n_tiles = -(-(n * TOP_K + N_EXPERTS * (MOE_BLOCK - 1)) // MOE_BLOCK)
    h, top_idx_t, gate_t, rank_t, counts = norm_router(x, g, sc, sh, router_w, router_b, segs)
    dest, row_tok, tile_e, n_used = moe_routing(top_idx_t, rank_t, counts[:, 0].astype(jnp.int32), n_tiles)
    y_rows = moe_experts(h, row_tok, tile_e, n_used, w_gu, b_gu, w_down, b_down)
    return moe_combine(x, y_rows, dest, gate_t.T, g2, segs)


def _pad_cols(w, width):
    return jnp.pad(w, ((0, 0), (0, width - w.shape[1])))


def trunk(x, c, p, segs):
    nseg = segs.nseg
    rows = -(-nseg // 8) * 8
    c_pad = jnp.pad(c, ((0, rows - nseg), (0, 0)))
    mod = ada_modulation(c_pad, p['ada_w'], p['ada_b'])[:, :nseg].reshape(DEPTH, nseg, 6, 1, D_MODEL)

    for layer in range(DEPTH):
        sh1, sc1, g1, sh2, sc2, g2 = [mod[layer, :, j] for j in range(6)]
        i = layer // 2
        n1 = p['norm1_g'][layer].reshape(1, -1)
        if layer % 2 == 0:
            w_in = p['ev_w_in'][i].astype(BF16)
            w_a = w_in[:, :1536]
            w_lr = _pad_cols(w_in[:, 1536:1568], 128)
            w_c = w_in[:, 1568:]
            a_proj, lr_proj, c_proj = norm_proj(x, n1, sc1, sh1, [w_a, w_lr, w_c], segs)
            o_f, o_b = gla_bidir(a_proj, lr_proj, p['gla_wa_f'][i], p['gla_ba_f'][i],
                                 p['gla_wa_b'][i], p['gla_ba_b'][i], segs)
            x = even_out(x, o_f, o_b, a_proj, c_proj, p['gla_norm_g'][i], p['conv_w'][i],
                         p['ev_w_out'][i].astype(BF16), g1, segs)
        else:
            w_in = p['od_w_in'][i].astype(BF16)
            q_proj, kv_proj, du = norm_proj(x, n1, sc1, sh1, [w_in[:, :512], w_in[:, 512:768], w_in[:, 768:]], segs)
            att = windowed_attention(q_proj, kv_proj, p['q_norm_g'][i], p['k_norm_g'][i], p['attn_sink'][i], segs)
            s5_params = (p['s5_lam_re'][i], p['s5_lam_im'][i], p['s5_log_step'][i],
                         p['s5_b_re'][i], p['s5_b_im'][i], p['s5_c_re'][i], p['s5_c_im'][i])
            ys = s5_conv(du, s5_params, segs)
            x = odd_out(x, att, du, ys, p['s5_d'][i], p['s5_glu_w'][i].astype(BF16), p['s5_glu_b'][i],
                        p['od_w_out'][i].astype(BF16), g1, segs)
        x = moe_layer(x, p['norm2_g'][layer].reshape(1, -1), sc2, sh2, g2,
                      p['router_w'][layer], p['router_b'][layer],
                      p['exp_w_gu'][layer].astype(BF16), p['exp_b_gu'][layer].reshape(N_EXPERTS, 1, -1),
                      p['exp_w_down'][layer].astype(BF16), p['exp_b_down'][layer].reshape(N_EXPERTS, 1, -1), segs)
    return x


def kernel(x_prompt, x_sample, c_prompt, c_sample, ada_w, ada_b, norm1_g, norm2_g, ev_w_in, ev_w_out, gla_wa_f, gla_ba_f, gla_wa_b, gla_ba_b, gla_norm_g, conv_w, od_w_in, od_w_out, q_norm_g, k_norm_g, attn_sink, s5_lam_re, s5_lam_im, s5_log_step, s5_b_re, s5_b_im, s5_c_re, s5_c_im, s5_d, s5_glu_w, s5_glu_b, router_w, router_b, exp_w_gu, exp_b_gu, exp_w_down, exp_b_down):
    p = dict(ada_w=ada_w, ada_b=ada_b, norm1_g=norm1_g, norm2_g=norm2_g,
             ev_w_in=ev_w_in, ev_w_out=ev_w_out, gla_wa_f=gla_wa_f, gla_ba_f=gla_ba_f,
             gla_wa_b=gla_wa_b, gla_ba_b=gla_ba_b, gla_norm_g=gla_norm_g, conv_w=conv_w,
             od_w_in=od_w_in, od_w_out=od_w_out, q_norm_g=q_norm_g, k_norm_g=k_norm_g,
             attn_sink=attn_sink, s5_lam_re=s5_lam_re, s5_lam_im=s5_lam_im, s5_log_step=s5_log_step,
             s5_b_re=s5_b_re, s5_b_im=s5_b_im, s5_c_re=s5_c_re, s5_c_im=s5_c_im, s5_d=s5_d,
             s5_glu_w=s5_glu_w, s5_glu_b=s5_glu_b, router_w=router_w, router_b=router_b,
             exp_w_gu=exp_w_gu, exp_b_gu=exp_b_gu, exp_w_down=exp_w_down, exp_b_down=exp_b_down)
    bp, tp, _ = x_prompt.shape
    bs, ts, _ = x_sample.shape
    seg = math.gcd(tp, ts)
    per_p, per_s = tp // seg, ts // seg
    assert bs == 1 and per_p == 1 and per_s == 2, "segment layout: prompt sequences of one segment, one sample sequence of two"
    nseg = bp * per_p + bs * per_s
    segs = Segs(seg=seg, nseg=nseg, join=bp * per_p + 1)
    x = jnp.concatenate([x_prompt.reshape(-1, D_MODEL), x_sample.reshape(-1, D_MODEL)], axis=0)
    c = jnp.concatenate([c_prompt, jnp.repeat(c_sample, per_s, axis=0)], axis=0)
    y = trunk(x, c, p, segs)
    n_p = bp * tp
    return y[:n_p].reshape(x_prompt.shape), y[n_p:].reshape(x_sample.shape)
```

```python
import functools
import math
from typing import NamedTuple

import numpy as np
import jax
import jax.numpy as jnp
from jax import lax
from jax.experimental import pallas as pl
from jax.experimental.pallas import tpu as pltpu

F32 = jnp.float32
BF16 = jnp.bfloat16

D_MODEL = 1024
DEPTH = 4
HALF = 512
HEAD_DIM = 64

GLA_HEADS = 4
GLA_DV = 128
GLA_DK = 64
GLA_RANK = 16
GLA_TAU = 16.0
GLA_CHUNK = 64

CONV_WIDTH = 3

ATT_Q_HEADS = 8
ATT_KV_HEADS = 2
ATT_WINDOW = 128

S5_GROUP = 16
S5_GROUPS = 32
S5_STATE = 64
S5_CHUNK = 16

N_EXPERTS = 32
TOP_K = 4
SWIGLU_LIMIT = 7.0
SWIGLU_ALPHA = 1.702
MOE_BLOCK = 512

NORM_EPS = 1e-6
NEG_BIG = -1e30

LANES = 128
ROW_SLABS = D_MODEL // LANES
GATHER_UNROLL = 8

VMEM_LIMIT_BYTES = 52 * 1024 * 1024

NT_DIMS = (((1,), (1,)), ((), ()))
TN_DIMS = (((0,), (0,)), ((), ()))


class Segs(NamedTuple):
    seg: int
    nseg: int
    join: int


def _params(n_axes=1):
    return pltpu.CompilerParams(dimension_semantics=("arbitrary",) * n_axes,
                                vmem_limit_bytes=VMEM_LIMIT_BYTES)


def _split_bf16(a):
    hi = a.astype(BF16)
    lo = (a - hi.astype(F32)).astype(BF16)
    return hi, lo


def _dot(a, b):
    return jnp.dot(a, b, preferred_element_type=F32)


def _dot_split(a, b):
    a_hi, a_lo = _split_bf16(a)
    b_hi, b_lo = _split_bf16(b)
    return _dot(a_hi, b_hi) + _dot(a_lo, b_hi) + _dot(a_hi, b_lo)


def _sublane_transpose8(vs):
    axis = vs[0].ndim - 2
    sub = lax.broadcasted_iota(jnp.int32, vs[0].shape, axis)
    vs = list(vs)
    for d in (4, 2, 1):
        low = (sub & d) == 0
        out = list(vs)
        for i in range(8):
            if i & d == 0:
                out[i] = jnp.where(low, vs[i], pltpu.roll(vs[i + d], d, axis=axis))
                out[i + d] = jnp.where(low, pltpu.roll(vs[i], 8 - d, axis=axis), vs[i + d])
        vs = out
    return vs


def _slabs_to_rows(slabs):
    rows = slabs.shape[0]
    groups = slabs.reshape(rows // 8, 8, ROW_SLABS, LANES)
    cols = _sublane_transpose8([groups[:, i] for i in range(8)])
    return jnp.concatenate([c.reshape(rows, LANES) for c in cols], axis=1)


def _rows_to_slabs(value):
    rows = value.shape[0]
    cols = [value[:, s * LANES:(s + 1) * LANES].reshape(rows // 8, 8, LANES) for s in range(ROW_SLABS)]
    tiles = _sublane_transpose8(cols)
    return jnp.stack(tiles, axis=1).reshape(rows, ROW_SLABS, LANES)


def _rms_mod(x, g, sc, sh):
    ms = jnp.mean(x * x, axis=-1, keepdims=True)
    h = x * lax.rsqrt(ms + NORM_EPS) * g
    return h * (1.0 + sc) + sh


def _ada_kernel(c_ref, w_ref, b_ref, o_ref):
    c = c_ref[...]
    s = c * jax.nn.sigmoid(c)
    o_ref[0] = _dot_split(s, w_ref[0]) + b_ref[0]


def ada_modulation(c_pad, ada_w, ada_b):
    rows = c_pad.shape[0]
    cols = 6 * D_MODEL
    tn = 1536
    return pl.pallas_call(
        _ada_kernel,
        grid=(DEPTH, cols // tn),
        in_specs=[pl.BlockSpec((rows, D_MODEL), lambda l, j: (0, 0)),
                  pl.BlockSpec((1, D_MODEL, tn), lambda l, j: (l, 0, j)),
                  pl.BlockSpec((1, 1, tn), lambda l, j: (l, 0, j))],
        out_specs=pl.BlockSpec((1, rows, tn), lambda l, j: (l, 0, j)),
        out_shape=jax.ShapeDtypeStruct((DEPTH, rows, cols), F32),
        compiler_params=_params(2),
        name="ada_modulation",
    )(c_pad, ada_w, ada_b.reshape(DEPTH, 1, cols))


def _norm_proj_kernel(x_ref, g_ref, sc_ref, sh_ref, *refs, n_w):
    h = _rms_mod(x_ref[...], g_ref[...], sc_ref[0], sh_ref[0]).astype(BF16)
    for w_ref, o_ref in zip(refs[:n_w], refs[n_w:]):
        o_ref[...] = _dot(h, w_ref[...]).astype(o_ref.dtype)


def norm_proj(x, g, sc, sh, weights, segs, tm=512):
    n = x.shape[0]
    n_w = len(weights)
    seg_of = lambda i: (i * tm) // segs.seg
    in_specs = [pl.BlockSpec((tm, D_MODEL), lambda i: (i, 0)),
                pl.BlockSpec((1, D_MODEL), lambda i: (0, 0)),
                pl.BlockSpec((1, 1, D_MODEL), lambda i: (seg_of(i), 0, 0)),
                pl.BlockSpec((1, 1, D_MODEL), lambda i: (seg_of(i), 0, 0))]
    in_specs += [pl.BlockSpec(w.shape, lambda i: (0, 0)) for w in weights]
    return pl.pallas_call(
        functools.partial(_norm_proj_kernel, n_w=n_w),
        grid=(n // tm,),
        in_specs=in_specs,
        out_specs=[pl.BlockSpec((tm, w.shape[1]), lambda i: (i, 0)) for w in weights],
        out_shape=[jax.ShapeDtypeStruct((n, w.shape[1]), F32) for w in weights],
        compiler_params=_params(),
        name="norm_proj",
    )(x, g, sc, sh, *weights)


def _gla_kernel(qk_f, v_f, lr_f, qk_b, v_b, lr_b, wa_f, ba_f, wa_b, ba_b, cum_f, cum_b,
                of_ref, ob_ref, st_f, st_b, *, segs, rows):
    i = pl.program_id(0)
    nb = pl.num_programs(0)
    ib = nb - 1 - i
    bps = segs.seg // rows
    seg_f = i // bps
    seg_b = ib // bps

    @pl.when((i % bps == 0) & (seg_f != segs.join))
    def _():
        st_f[...] = jnp.zeros_like(st_f)

    @pl.when((ib % bps == bps - 1) & (seg_b + 1 != segs.join))
    def _():
        st_b[...] = jnp.zeros_like(st_b)

    L = GLA_CHUNK
    scale = GLA_DK ** -0.5
    n_chunks = rows // L
    dkw = GLA_HEADS * GLA_DK

    def direction(qk_ref, v_ref, lr_ref, wa_ref, ba_ref, cum_ref, st_ref, o_ref, reverse):
        qk = qk_ref[...]
        q = qk[:, :dkw] * scale
        k = qk[:, dkw:]
        v = v_ref[...].astype(BF16)
        pre = _dot_split(lr_ref[...], wa_ref[...]) + ba_ref[...]
        log_sig = jnp.minimum(pre, 0.0) - jnp.log(1.0 + jnp.exp(-jnp.abs(pre)))
        g_hi, g_lo = _split_bf16(log_sig * (1.0 / GLA_TAU))
        cum = cum_ref[...]
        b = _dot(cum, g_hi) + _dot(cum, g_lo)
        edge = 0 if reverse else L - 1
        b_edge = jnp.concatenate([jnp.broadcast_to(b[c * L + edge:c * L + edge + 1, :], (L, dkw))
                                  for c in range(n_chunks)], axis=0)
        qe = (q * jnp.exp(b)).astype(BF16)
        ke = (k * jnp.exp(-b)).astype(BF16)
        kd = (k * jnp.exp(b_edge - b)).astype(BF16)
        d_edge = jnp.exp(b_edge)
        mask = cum != 0
        order = range(n_chunks - 1, -1, -1) if reverse else range(n_chunks)
        outs = []
        for h in range(GLA_HEADS):
            ks = slice(h * GLA_DK, (h + 1) * GLA_DK)
            qeh, keh, kdh = qe[:, ks], ke[:, ks], kd[:, ks]
            vh = v[:, h * GLA_DV:(h + 1) * GLA_DV]
            att = lax.dot_general(qeh, keh, NT_DIMS, preferred_element_type=F32)
            o = _dot(jnp.where(mask, att, 0.0).astype(BF16), vh)
            st = st_ref[h]
            inter = [None] * n_chunks
            for c in order:
                rs = slice(c * L, (c + 1) * L)
                inter[c] = lax.dot_general(qeh[rs], st.astype(BF16), NT_DIMS, preferred_element_type=F32)
                st = st * d_edge[c * L:c * L + 1, ks] + lax.dot_general(vh[rs], kdh[rs], TN_DIMS,
                                                                        preferred_element_type=F32)
            st_ref[h] = st
            outs.append(o + jnp.concatenate(inter, axis=0))
        o_ref[...] = jnp.concatenate(outs, axis=1)

    direction(qk_f, v_f, lr_f, wa_f, ba_f, cum_f, st_f, of_ref, False)
    direction(qk_b, v_b, lr_b, wa_b, ba_b, cum_b, st_b, ob_ref, True)


def gla_bidir(a_proj, lr_proj, wa_f, ba_f, wa_b, ba_b, segs, rows=512):
    n = a_proj.shape[0]
    nb = n // rows
    lanes = lr_proj.shape[1]
    wa_f_pad = jnp.zeros((lanes, GLA_HEADS * GLA_DK), F32).at[:GLA_RANK].set(wa_f)
    wa_b_pad = jnp.zeros((lanes, GLA_HEADS * GLA_DK), F32).at[GLA_RANK:2 * GLA_RANK].set(wa_b)
    t_io = jnp.arange(rows, dtype=jnp.int32)
    same_chunk = (t_io[:, None] // GLA_CHUNK) == (t_io[None, :] // GLA_CHUNK)
    cum_f = (same_chunk & (t_io[None, :] <= t_io[:, None])).astype(BF16)
    cum_b = (same_chunk & (t_io[None, :] >= t_io[:, None])).astype(BF16)
    fwd = lambda i: (i, 0)
    bwd = lambda i: (nb - 1 - i, 0)
    const = lambda i: (0, 0)
    spec = lambda width, col, row_map: pl.BlockSpec((rows, width), lambda i: (row_map(i)[0], col))
    return pl.pallas_call(
        functools.partial(_gla_kernel, segs=segs, rows=rows),
        grid=(nb,),
        in_specs=[spec(512, 0, fwd), spec(512, 1, fwd), pl.BlockSpec((rows, lanes), fwd),
                  spec(512, 0, bwd), spec(512, 1, bwd), pl.BlockSpec((rows, lanes), bwd),
                  pl.BlockSpec(wa_f_pad.shape, const), pl.BlockSpec((1, 256), const),
                  pl.BlockSpec(wa_b_pad.shape, const), pl.BlockSpec((1, 256), const),
                  pl.BlockSpec((rows, rows), const), pl.BlockSpec((rows, rows), const)],
        out_specs=[pl.BlockSpec((rows, HALF), fwd), pl.BlockSpec((rows, HALF), bwd)],
        out_shape=[jax.ShapeDtypeStruct((n, HALF), F32)] * 2,
        scratch_shapes=[pltpu.VMEM((GLA_HEADS, GLA_DV, GLA_DK), F32)] * 2,
        compiler_params=_params(),
        name="gla_bidir",
    )(a_proj, a_proj, lr_proj, a_proj, a_proj, lr_proj,
      wa_f_pad, ba_f.reshape(1, -1), wa_b_pad, ba_b.reshape(1, -1), cum_f, cum_b)


def _even_out_kernel(x_ref, of_ref, ob_ref, og_ref, c_ref, cprev_ref, cnext_ref, ng_ref, cw_ref,
                     w_ref, g1_ref, o_ref, *, segs, tm):
    i = pl.program_id(0)
    tps = segs.seg // tm
    seg = i // tps
    has_prev = jnp.logical_not((i % tps == 0) & (seg != segs.join))
    has_next = jnp.logical_not((i % tps == tps - 1) & (seg + 1 != segs.join))

    o = of_ref[...] + ob_ref[...]
    og = og_ref[...]
    parts = []
    for h in range(GLA_HEADS):
        oh = o[:, h * GLA_DV:(h + 1) * GLA_DV]
        ms = jnp.mean(oh * oh, axis=-1, keepdims=True)
        parts.append(oh * lax.rsqrt(ms + NORM_EPS) * ng_ref[...])
    a_out = jnp.concatenate(parts, axis=1) * (og * jax.nn.sigmoid(og))

    c = c_ref[...]
    bg = c[:, :HALF]
    u = c[:, HALF:2 * HALF] * c[:, 2 * HALF:]
    cp = cprev_ref[7:8, :]
    cn = cnext_ref[0:1, :]
    u_prev_edge = jnp.where(has_prev, cp[:, HALF:2 * HALF] * cp[:, 2 * HALF:], 0.0)
    u_next_edge = jnp.where(has_next, cn[:, HALF:2 * HALF] * cn[:, 2 * HALF:], 0.0)
    row = lax.broadcasted_iota(jnp.int32, (tm, 1), 0)
    u_prev = jnp.where(row == 0, u_prev_edge, pltpu.roll(u, 1, axis=0))
    u_next = jnp.where(row == tm - 1, u_next_edge, pltpu.roll(u, tm - 1, axis=0))
    cw = cw_ref[...]
    b_out = bg * (cw[0:1, :] * u_prev + cw[1:2, :] * u + cw[2:3, :] * u_next)

    w = w_ref[...]
    mix = _dot(a_out.astype(BF16), w[:HALF, :]) + _dot(b_out.astype(BF16), w[HALF:, :])
    o_ref[...] = x_ref[...] + g1_ref[0] * mix


def even_out(x, o_f, o_b, a_proj, c_proj, norm_g, conv_w, w_out, g1, segs, tm=512):
    n = x.shape[0]
    last8 = n // 8 - 1
    seg_of = lambda i: (i * tm) // segs.seg
    return pl.pallas_call(
        functools.partial(_even_out_kernel, segs=segs, tm=tm),
        grid=(n // tm,),
        in_specs=[pl.BlockSpec((tm, D_MODEL), lambda i: (i, 0)),
                  pl.BlockSpec((tm, HALF), lambda i: (i, 0)),
                  pl.BlockSpec((tm, HALF), lambda i: (i, 0)),
                  pl.BlockSpec((tm, HALF), lambda i: (i, 2)),
                  pl.BlockSpec((tm, 3 * HALF), lambda i: (i, 0)),
                  pl.BlockSpec((8, 3 * HALF), lambda i: (jnp.maximum(i * (tm // 8) - 1, 0), 0)),
                  pl.BlockSpec((8, 3 * HALF), lambda i: (jnp.minimum((i + 1) * (tm // 8), last8), 0)),
                  pl.BlockSpec((1, GLA_DV), lambda i: (0, 0)),
                  pl.BlockSpec((CONV_WIDTH, HALF), lambda i: (0, 0)),
                  pl.BlockSpec((D_MODEL, D_MODEL), lambda i: (0, 0)),
                  pl.BlockSpec((1, 1, D_MODEL), lambda i: (seg_of(i), 0, 0))],
        out_specs=pl.BlockSpec((tm, D_MODEL), lambda i: (i, 0)),
        out_shape=jax.ShapeDtypeStruct((n, D_MODEL), F32),
        compiler_params=_params(),
        name="even_out",
    )(x, o_f, o_b, a_proj, c_proj, c_proj, c_proj, norm_g.reshape(1, -1), conv_w, w_out, g1)


def _attn_kernel(q_ref, kv_ref, kvp_ref, kvn_ref, qg_ref, kg_ref, sink_ref, o_ref, *, segs, tq):
    i = pl.program_id(0)
    W = ATT_WINDOW
    tps = segs.seg // tq
    seg = i // tps
    first = (i % tps == 0) & (seg != segs.join)
    last = (i % tps == tps - 1) & (seg + 1 != segs.join)

    kv_all = jnp.concatenate([kvp_ref[...], kv_ref[...], kvn_ref[...]], axis=0)
    kvw = ATT_KV_HEADS * HEAD_DIM
    k_heads, v_heads = [], []
    for h in range(ATT_KV_HEADS):
        kh = kv_all[:, h * HEAD_DIM:(h + 1) * HEAD_DIM]
        ms = jnp.mean(kh * kh, axis=-1, keepdims=True)
        k_heads.append((kh * lax.rsqrt(ms + NORM_EPS) * kg_ref[...]).astype(BF16))
        v_heads.append(kv_all[:, kvw + h * HEAD_DIM:kvw + (h + 1) * HEAD_DIM].astype(BF16))

    t_io = lax.broadcasted_iota(jnp.int32, (W, 3 * W), 0)
    j_io = lax.broadcasted_iota(jnp.int32, (W, 3 * W), 1)
    rel = j_io - W - t_io
    dist = jnp.abs(rel)
    in_window = dist <= W
    dist_f = dist.astype(F32)
    group = ATT_Q_HEADS // ATT_KV_HEADS
    n_blk = tq // W
    for blk in range(n_blk):
        valid = in_window
        if blk == 0:
            valid = valid & ((j_io >= W) | jnp.logical_not(first))
        if blk == n_blk - 1:
            valid = valid & ((j_io < 2 * W) | jnp.logical_not(last))
        q = q_ref[blk * W:(blk + 1) * W, :]
        outs = []
        for hq in range(ATT_Q_HEADS):
            kvh = hq // group
            qh = q[:, hq * HEAD_DIM:(hq + 1) * HEAD_DIM]
            ms = jnp.mean(qh * qh, axis=-1, keepdims=True)
            qn = (qh * lax.rsqrt(ms + NORM_EPS) * qg_ref[...] * (HEAD_DIM ** -0.5)).astype(BF16)
            kh = k_heads[kvh][blk * W:(blk + 3) * W, :]
            vh = v_heads[kvh][blk * W:(blk + 3) * W, :]
            s = lax.dot_general(qn, kh, NT_DIMS, preferred_element_type=F32)
            slope = 2.0 ** (-8.0 * (hq + 1) / ATT_Q_HEADS)
            s = jnp.where(valid, s - slope * dist_f, NEG_BIG)
            sk = sink_ref[hq]
            m = jnp.maximum(jnp.max(s, axis=-1, keepdims=True), sk)
            p = jnp.exp(s - m)
            denom = jnp.sum(p, axis=-1, keepdims=True) + jnp.exp(sk - m)
            outs.append(_dot(p.astype(BF16), vh) / denom)
        o_ref[blk * W:(blk + 1) * W, :] = jnp.concatenate(outs, axis=1)


def windowed_attention(q_proj, kv_proj, q_norm_g, k_norm_g, sink, segs, tq=512):
    n = q_proj.shape[0]
    W = ATT_WINDOW
    r = tq // W
    last = n // W - 1
    kvw = 2 * ATT_KV_HEADS * HEAD_DIM
    return pl.pallas_call(
        functools.partial(_attn_kernel, segs=segs, tq=tq),
        grid=(n // tq,),
        in_specs=[pl.BlockSpec((tq, HALF), lambda i: (i, 0)),
                  pl.BlockSpec((tq, kvw), lambda i: (i, 0)),
                  pl.BlockSpec((W, kvw), lambda i: (jnp.maximum(i * r - 1, 0), 0)),
                  pl.BlockSpec((W, kvw), lambda i: (jnp.minimum((i + 1) * r, last), 0)),
                  pl.BlockSpec((1, HEAD_DIM), lambda i: (0, 0)),
                  pl.BlockSpec((1, HEAD_DIM), lambda i: (0, 0)),
                  pl.BlockSpec(memory_space=pltpu.SMEM)],
        out_specs=pl.BlockSpec((tq, HALF), lambda i: (i, 0)),
        out_shape=jax.ShapeDtypeStruct((n, HALF), F32),
        compiler_params=_params(),
        name="windowed_attention",
    )(q_proj, kv_proj, kv_proj, kv_proj, q_norm_g.reshape(1, -1), k_norm_g.reshape(1, -1), sink)


def s5_tables(lam_re, lam_im, log_step, b_re, b_im, c_re, c_im, *, n_inner):
    L = S5_CHUNK
    dt = jnp.exp(log_step)[:, :, None]
    lr, li = lam_re, lam_im
    mag = jnp.exp(lr * dt)
    ar, ai = mag * jnp.cos(li * dt), mag * jnp.sin(li * dt)
    den = lr * lr + li * li
    zr = ((ar - 1.0) * lr + ai * li) / den
    zi = (ai * lr - (ar - 1.0) * li) / den
    bbr = zr[..., None] * b_re - zi[..., None] * b_im
    bbi = zr[..., None] * b_im + zi[..., None] * b_re
    tau = jnp.arange(L + 1, dtype=F32)[:, None, None, None]
    pmag = jnp.exp(lr[None] * dt[None] * tau)
    pang = li[None] * dt[None] * tau
    pr, pi = pmag * jnp.cos(pang), pmag * jnp.sin(pang)

    hp = lax.Precision.HIGHEST
    car = c_re[None] * pr[:, :, :, None, :] - c_im[None] * pi[:, :, :, None, :]
    cai = c_re[None] * pi[:, :, :, None, :] + c_im[None] * pr[:, :, :, None, :]
    kern = (jnp.einsum('ldgop,dgpi->ldgoi', car, bbr, precision=hp)
            - jnp.einsum('ldgop,dgpi->ldgoi', cai, bbi, precision=hp))
    s_idx = jnp.arange(L)[:, None]
    t_idx = jnp.arange(L)[None, :]
    lag_f = jnp.clip(t_idx - s_idx, 0, L)
    lag_b = jnp.clip(s_idx - t_idx, 0, L)
    kf = jnp.where((t_idx >= s_idx)[:, :, None, None, None], kern[:, 0][lag_f], 0.0)
    kb = jnp.where((s_idx >= t_idx)[:, :, None, None, None], kern[:, 1][lag_b], 0.0)
    m = (kf + kb).transpose(2, 0, 4, 1, 3).reshape(S5_GROUPS, L * S5_GROUP, L * S5_GROUP)

    abr = pr[..., None] * bbr[None] - pi[..., None] * bbi[None]
    abi = pr[..., None] * bbi[None] + pi[..., None] * bbr[None]
    e_f = L - 1 - jnp.arange(L)
    e_b = jnp.arange(L)
    to_rows = lambda a: a.transpose(1, 0, 3, 2).reshape(S5_GROUPS, L * S5_GROUP, S5_STATE)
    wp = jnp.concatenate([to_rows(abr[e_f, 0]), to_rows(abi[e_f, 0]),
                          to_rows(abr[e_b, 1]), to_rows(abi[e_b, 1])], axis=-1)

    o_f = jnp.arange(L) + 1
    o_b = L - jnp.arange(L)
    to_cols = lambda a: a.transpose(1, 3, 0, 2).reshape(S5_GROUPS, S5_STATE, L * S5_GROUP)
    wc = jnp.concatenate([to_cols(car[o_f, 0]), -to_cols(cai[o_f, 0]),
                          to_cols(car[o_b, 1]), -to_cols(cai[o_b, 1])], axis=1)

    steps = jnp.arange(n_inner + 1, dtype=F32)[:, None, None, None] * float(L)
    qmag = jnp.exp(lr[None] * dt[None] * steps)
    qang = li[None] * dt[None] * steps
    qr, qi = qmag * jnp.cos(qang), qmag * jnp.sin(qang)
    form_a = jnp.concatenate([qr, qr], -1)
    form_b = jnp.concatenate([-qi, qi], -1)
    dpow = jnp.stack([form_a[:, 0], form_b[:, 0], form_a[:, 1], form_b[:, 1]], axis=0)
    dpow = dpow.transpose(2, 0, 1, 3)
    return m.astype(BF16), wp.astype(BF16), wc, dpow


def _s5_kernel(x_ref, m_ref, wp_ref, wc_ref, dpow_ref, y_ref, pf_scr, pb_scr, sf_scr, sb_scr, *, segs, n_sub):
    P2 = 2 * S5_STATE
    n_inner = segs.seg // S5_CHUNK // n_sub
    rows_step = segs.nseg * n_sub
    x = x_ref[0]
    y_ref[0] = _dot(x, m_ref[0])
    p = _dot(x, wp_ref[0])
    pf_scr[...] = p[:, :P2]
    pb_scr[...] = p[:, P2:]

    def cmul(s, s_swapped, form_a, form_b):
        return s * form_a + s_swapped * form_b

    def swap(s):
        return pltpu.roll(s, S5_STATE, axis=1)

    def power(form, i):
        return dpow_ref[0, form, pl.ds(i, 1), :]

    d1 = [power(f, 1) for f in range(4)]
    dn = [power(f, n_inner) for f in range(4)]

    def block(i):
        return pl.ds(pl.multiple_of(i * rows_step, 8), rows_step)

    def local_step(i, carry):
        s_f, s_b = carry
        rf, rb = block(i), block(n_inner - 1 - i)
        sf_scr[rf, :] = s_f
        sb_scr[rb, :] = s_b
        return (cmul(s_f, swap(s_f), d1[0], d1[1]) + pf_scr[rf, :],
                cmul(s_b, swap(s_b), d1[2], d1[3]) + pb_scr[rb, :])
    zeros = jnp.zeros((rows_step, P2), F32)
    end_f, end_b = lax.fori_loop(0, n_inner, local_step, (zeros, zeros))

    r_io = lax.broadcasted_iota(jnp.int32, (rows_step, 1), 0)
    j_io, seg_io = r_io % n_sub, r_io // n_sub
    takes_prev = jnp.logical_not((j_io == 0) & (seg_io != segs.join))
    takes_next = jnp.logical_not((j_io == n_sub - 1) & (seg_io + 1 != segs.join))
    end_f_prev = pltpu.roll(end_f, 1, axis=0)
    end_b_next = pltpu.roll(end_b, rows_step - 1, axis=0)
    c_f, c_b = zeros, zeros
    longest = n_sub * (2 if segs.join >= 0 else 1)
    for _ in range(longest - 1):
        c_prev = pltpu.roll(c_f, 1, axis=0)
        c_f = jnp.where(takes_prev, cmul(c_prev, swap(c_prev), dn[0], dn[1]) + end_f_prev, 0.0)
        c_next = pltpu.roll(c_b, rows_step - 1, axis=0)
        c_b = jnp.where(takes_next, cmul(c_next, swap(c_next), dn[2], dn[3]) + end_b_next, 0.0)

    c_f_sw, c_b_sw = swap(c_f), swap(c_b)

    def fix_step(i, carry):
        rf, rb = block(i), block(n_inner - 1 - i)
        sf_scr[rf, :] += cmul(c_f, c_f_sw, power(0, i), power(1, i))
        sb_scr[rb, :] += cmul(c_b, c_b_sw, power(2, i), power(3, i))
        return carry
    lax.fori_loop(0, n_inner, fix_step, 0)

    wc_hi, wc_lo = _split_bf16(wc_ref[0])
    acc = y_ref[0]
    for s_scr, rows in ((sf_scr, slice(0, P2)), (sb_scr, slice(P2, 2 * P2))):
        s_hi, s_lo = _split_bf16(s_scr[...])
        acc = acc + _dot(s_hi, wc_hi[rows]) + _dot(s_lo, wc_hi[rows]) + _dot(s_hi, wc_lo[rows])
    y_ref[0] = acc


def s5_conv(du, params, segs, n_sub=8):
    n = du.shape[0]
    L = S5_CHUNK
    rows = n // L
    width = L * S5_GROUP
    n_inner = segs.seg // L // n_sub
    m, wp, wc, dpow = s5_tables(*params, n_inner=n_inner)
    xg = du.astype(BF16).reshape(segs.nseg, n_sub, n_inner, L, S5_GROUPS, S5_GROUP)
    xg = xg.transpose(4, 2, 0, 1, 3, 5).reshape(S5_GROUPS, rows, width)
    grp = lambda g: (g, 0, 0)
    yg = pl.pallas_call(
        functools.partial(_s5_kernel, segs=segs, n_sub=n_sub),
        grid=(S5_GROUPS,),
        in_specs=[pl.BlockSpec((1, rows, width), grp),
                  pl.BlockSpec((1, width, width), grp),
                  pl.BlockSpec((1, width, 4 * S5_STATE), grp),
                  pl.BlockSpec((1, 4 * S5_STATE, width), grp),
                  pl.BlockSpec((1, 4, n_inner + 1, 2 * S5_STATE), lambda g: (g, 0, 0, 0))],
        out_specs=pl.BlockSpec((1, rows, width), grp),
        out_shape=jax.ShapeDtypeStruct((S5_GROUPS, rows, width), F32),
        scratch_shapes=[pltpu.VMEM((rows, 2 * S5_STATE), F32)] * 4,
        compiler_params=_params(),
        name="s5_conv",
    )(xg, m, wp, wc, dpow)
    yg = yg.reshape(S5_GROUPS, n_inner, segs.nseg, n_sub, L, S5_GROUP)
    return yg.transpose(2, 3, 1, 4, 0, 5).reshape(n, HALF)


def _odd_out_kernel(x_ref, att_ref, du_ref, ys_ref, dsk_ref, gw_ref, gb_ref, w_ref, g1_ref, o_ref):
    y = dsk_ref[...] * du_ref[...] + ys_ref[...]
    z = 0.5 * y * (1.0 + jnp.tanh(math.sqrt(2.0 / math.pi) * (y + 0.044715 * (y * y * y))))
    gate = jax.nn.sigmoid(_dot(z.astype(BF16), gw_ref[...]) + gb_ref[...])
    d_out = z * gate
    w = w_ref[...]
    mix = _dot(att_ref[...].astype(BF16), w[:HALF, :]) + _dot(d_out.astype(BF16), w[HALF:, :])
    o_ref[...] = x_ref[...] + g1_ref[0] * mix


def odd_out(x, att, du, ys, d_skip, glu_w, glu_b, w_out, g1, segs, tm=512):
    n = x.shape[0]
    seg_of = lambda i: (i * tm) // segs.seg
    row = lambda i: (i, 0)
    const = lambda i: (0, 0)
    return pl.pallas_call(
        _odd_out_kernel,
        grid=(n // tm,),
        in_specs=[pl.BlockSpec((tm, D_MODEL), row), pl.BlockSpec((tm, HALF), row),
                  pl.BlockSpec((tm, HALF), row), pl.BlockSpec((tm, HALF), row),
                  pl.BlockSpec((1, HALF), const), pl.BlockSpec((HALF, HALF), const),
                  pl.BlockSpec((1, HALF), const), pl.BlockSpec((D_MODEL, D_MODEL), const),
                  pl.BlockSpec((1, 1, D_MODEL), lambda i: (seg_of(i), 0, 0))],
        out_specs=pl.BlockSpec((tm, D_MODEL), row),
        out_shape=jax.ShapeDtypeStruct((n, D_MODEL), F32),
        compiler_params=_params(),
        name="odd_out",
    )(x, att, du, ys, d_skip.reshape(1, -1), glu_w, glu_b.reshape(1, -1), w_out, g1)


def _norm_router_kernel(x_ref, g_ref, sc_ref, sh_ref, wr_ref, br_ref, before_ref,
                        h_ref, idx_ref, gate_ref, rank_ref, count_ref):
    @pl.when(pl.program_id(0) == 0)
    def _():
        count_ref[...] = jnp.zeros_like(count_ref)

    h = _rms_mod(x_ref[...], g_ref[...], sc_ref[0], sh_ref[0])
    h_ref[...] = _rows_to_slabs(h)
    h_hi, h_lo = _split_bf16(h)
    w_hi, w_lo = _split_bf16(wr_ref[...])
    nt = lambda a, b: lax.dot_general(a, b, NT_DIMS, preferred_element_type=F32)
    logits = nt(w_hi, h_hi) + nt(w_lo, h_hi) + nt(w_hi, h_lo) + br_ref[...]
    e_io = lax.broadcasted_iota(jnp.int32, logits.shape, 0)
    tops, picks = [], []
    for k in range(TOP_K):
        m = jnp.max(logits, axis=0, keepdims=True)
        idx = jnp.min(jnp.where(logits == m, e_io, N_EXPERTS), axis=0, keepdims=True)
        idx_ref[k:k + 1, :] = idx
        pick = e_io == idx
        logits = jnp.where(pick, -jnp.inf, logits)
        tops.append(m)
        picks.append(pick)
    es = [jnp.exp(t - tops[0]) for t in tops]
    total = es[0] + es[1] + es[2] + es[3]
    for k in range(TOP_K):
        gate_ref[k:k + 1, :] = es[k] / total

    chosen = (picks[0] | picks[1] | picks[2] | picks[3]).astype(F32)
    rank = _dot(chosen.astype(BF16), before_ref[...]) + count_ref[:, 0:1]
    for k in range(TOP_K):
        rank_ref[k:k + 1, :] = jnp.sum(jnp.where(picks[k], rank, 0.0), axis=0, keepdims=True).astype(jnp.int32)
    count_ref[...] = count_ref[...] + jnp.sum(chosen, axis=1, keepdims=True)


def norm_router(x, g, sc, sh, router_w, router_b, segs, tm=512):
    n = x.shape[0]
    seg_of = lambda i: (i * tm) // segs.seg
    t_io = jnp.arange(tm, dtype=jnp.int32)
    before = (t_io[:, None] < t_io[None, :]).astype(BF16)
    return pl.pallas_call(
        _norm_router_kernel,
        grid=(n // tm,),
        in_specs=[pl.BlockSpec((tm, D_MODEL), lambda i: (i, 0)),
                  pl.BlockSpec((1, D_MODEL), lambda i: (0, 0)),
                  pl.BlockSpec((1, 1, D_MODEL), lambda i: (seg_of(i), 0, 0)),
                  pl.BlockSpec((1, 1, D_MODEL), lambda i: (seg_of(i), 0, 0)),
                  pl.BlockSpec((N_EXPERTS, D_MODEL), lambda i: (0, 0)),
                  pl.BlockSpec((N_EXPERTS, 1), lambda i: (0, 0)),
                  pl.BlockSpec((tm, tm), lambda i: (0, 0))],
        out_specs=[pl.BlockSpec((tm, ROW_SLABS, LANES), lambda i: (i, 0, 0)),
                   pl.BlockSpec((TOP_K, tm), lambda i: (0, i)),
                   pl.BlockSpec((TOP_K, tm), lambda i: (0, i)),
                   pl.BlockSpec((TOP_K, tm), lambda i: (0, i)),
                   pl.BlockSpec((N_EXPERTS, LANES), lambda i: (0, 0))],
        out_shape=[jax.ShapeDtypeStruct((n, ROW_SLABS, LANES), F32),
                   jax.ShapeDtypeStruct((TOP_K, n), jnp.int32),
                   jax.ShapeDtypeStruct((TOP_K, n), F32),
                   jax.ShapeDtypeStruct((TOP_K, n), jnp.int32),
                   jax.ShapeDtypeStruct((N_EXPERTS, LANES), F32)],
        compiler_params=_params(),
        name="norm_router",
    )(x, g, sc, sh, router_w.T, router_b.reshape(-1, 1), before)


def moe_routing(top_idx_t, rank_t, counts, n_tiles):
    n = top_idx_t.shape[1]
    padded = (counts + MOE_BLOCK - 1) // MOE_BLOCK * MOE_BLOCK
    padded_end = jnp.cumsum(padded)
    padded_start = padded_end - padded
    experts = jnp.arange(N_EXPERTS, dtype=jnp.int32)
    start_of = jnp.sum(jnp.where(top_idx_t[:, :, None] == experts, padded_start, 0), axis=-1)
    dest = start_of + rank_t
    n_rows = n_tiles * MOE_BLOCK
    tok = jnp.broadcast_to(jnp.arange(n, dtype=jnp.int32)[None], dest.shape)
    row_tok = jnp.zeros((n_rows,), jnp.int32).at[dest.reshape(-1)].set(tok.reshape(-1), unique_indices=True)
    n_used = (padded_end[-1] // MOE_BLOCK).astype(jnp.int32)
    tile_start = jnp.arange(n_tiles, dtype=jnp.int32) * MOE_BLOCK
    tile_e = jnp.sum((padded_end[None, :] <= tile_start[:, None]).astype(jnp.int32), axis=1)
    tile_e = jnp.minimum(tile_e, N_EXPERTS - 1)
    last_e = tile_e[jnp.maximum(n_used - 1, 0)]
    tile_e = jnp.where(jnp.arange(n_tiles) < n_used, tile_e, last_e)
    return dest, row_tok, tile_e, n_used.reshape(1)


def _start_row_gather(src_hbm, dst_vmem, sem, idx_ref, n_rows, n_priorities=1):
    def issue(group, carry):
        for u in range(GATHER_UNROLL):
            r = group * GATHER_UNROLL + u
            copy = pltpu.make_async_copy(src_hbm.at[pl.ds(idx_ref[0, 0, r], 1)], dst_vmem.at[pl.ds(r, 1)], sem)
            copy.start(priority=u % n_priorities)
        return carry
    lax.fori_loop(0, n_rows // GATHER_UNROLL, issue, 0)


def _wait_row_gather(src_hbm, dst_vmem, sem, n_rows):
    pltpu.make_async_copy(src_hbm.at[pl.ds(0, n_rows)], dst_vmem, sem).wait()


def _expert_kernel(tile_e_ref, n_used_ref, rows_ref, rows_next_ref, h_hbm, wgu_ref, bgu_ref, wd_ref, bd_ref,
                   y_ref, xbuf, sems):
    t = pl.program_id(0)
    n_used = n_used_ref[0]
    slot = t % 2

    @pl.when((t == 0) & (n_used > 0))
    def _():
        _start_row_gather(h_hbm, xbuf.at[0], sems.at[0], rows_ref, MOE_BLOCK)

    @pl.when(t + 1 < n_used)
    def _():
        _start_row_gather(h_hbm, xbuf.at[1 - slot], sems.at[1 - slot], rows_next_ref, MOE_BLOCK)

    @pl.when(t < n_used)
    def _():
        _wait_row_gather(h_hbm, xbuf.at[slot], sems.at[slot], MOE_BLOCK)
        x = _slabs_to_rows(xbuf[slot]).astype(BF16)
        gu = _dot(x, wgu_ref[0]) + bgu_ref[0]
        d_ff = gu.shape[1] // 2
        g_ = jnp.minimum(gu[:, :d_ff], SWIGLU_LIMIT)
        u_ = jnp.clip(gu[:, d_ff:], -SWIGLU_LIMIT, SWIGLU_LIMIT)
        hh = g_ * jax.nn.sigmoid(SWIGLU_ALPHA * g_) * (u_ + 1.0)
        y_ref[...] = _rows_to_slabs(_dot(hh.astype(BF16), wd_ref[0]) + bd_ref[0])

    @pl.when(t >= n_used)
    def _():
        y_ref[...] = jnp.zeros_like(y_ref)


def moe_experts(h, row_tok, tile_e, n_used, w_gu, b_gu, w_down, b_down):
    n_tiles = tile_e.shape[0]
    d_ff2 = w_gu.shape[2]
    grid_spec = pltpu.PrefetchScalarGridSpec(
        num_scalar_prefetch=2,
        grid=(n_tiles,),
        in_specs=[pl.BlockSpec((1, 1, MOE_BLOCK), lambda t, te, nu: (t, 0, 0), memory_space=pltpu.SMEM),
                  pl.BlockSpec((1, 1, MOE_BLOCK), lambda t, te, nu: (jnp.minimum(t + 1, n_tiles - 1), 0, 0),
                               memory_space=pltpu.SMEM),
                  pl.BlockSpec(memory_space=pl.ANY),
                  pl.BlockSpec((1, D_MODEL, d_ff2), lambda t, te, nu: (te[t], 0, 0)),
                  pl.BlockSpec((1, 1, d_ff2), lambda t, te, nu: (te[t], 0, 0)),
                  pl.BlockSpec((1, d_ff2 // 2, D_MODEL), lambda t, te, nu: (te[t], 0, 0)),
                  pl.BlockSpec((1, 1, D_MODEL), lambda t, te, nu: (te[t], 0, 0))],
        out_specs=pl.BlockSpec((MOE_BLOCK, ROW_SLABS, LANES), lambda t, te, nu: (t, 0, 0)),
        scratch_shapes=[pltpu.VMEM((2, MOE_BLOCK, ROW_SLABS, LANES), F32), pltpu.SemaphoreType.DMA((2,))],
    )
    rows = row_tok.reshape(n_tiles, 1, MOE_BLOCK)
    return pl.pallas_call(
        _expert_kernel,
        grid_spec=grid_spec,
        out_shape=jax.ShapeDtypeStruct((n_tiles * MOE_BLOCK, ROW_SLABS, LANES), F32),
        compiler_params=_params(),
        name="moe_experts",
    )(tile_e, n_used, rows, rows, h, w_gu, b_gu, w_down, b_down)


def _combine_kernel(dest_ref, dest_next_ref, y_hbm, x_ref, gate_ref, g2_ref, o_ref, ybuf, sems, *, tm):
    i = pl.program_id(0)
    slot = i % 2
    n_rows = TOP_K * tm

    @pl.when(i == 0)
    def _():
        _start_row_gather(y_hbm, ybuf.at[0], sems.at[0], dest_ref, n_rows, n_priorities=2)

    @pl.when(i + 1 < pl.num_programs(0))
    def _():
        _start_row_gather(y_hbm, ybuf.at[1 - slot], sems.at[1 - slot], dest_next_ref, n_rows, n_priorities=2)

    _wait_row_gather(y_hbm, ybuf.at[slot], sems.at[slot], n_rows)
    gate = gate_ref[...]
    acc = gate[:, 0:1] * _slabs_to_rows(ybuf[slot, 0:tm])
    for k in range(1, TOP_K):
        acc = acc + gate[:, k:k + 1] * _slabs_to_rows(ybuf[slot, k * tm:(k + 1) * tm])
    o_ref[...] = x_ref[...] + g2_ref[0] * acc


def moe_combine(x, y_rows, dest, gate, g2, segs, tm=256):
    n = x.shape[0]
    nt = n // tm
    dest_tiles = dest.reshape(TOP_K, nt, tm).transpose(1, 0, 2).reshape(nt, 1, TOP_K * tm)
    seg_of = lambda i: (i * tm) // segs.seg
    return pl.pallas_call(
        functools.partial(_combine_kernel, tm=tm),
        grid=(nt,),
        in_specs=[pl.BlockSpec((1, 1, TOP_K * tm), lambda i: (i, 0, 0), memory_space=pltpu.SMEM),
                  pl.BlockSpec((1, 1, TOP_K * tm), lambda i: (jnp.minimum(i + 1, nt - 1), 0, 0),
                               memory_space=pltpu.SMEM),
                  pl.BlockSpec(memory_space=pl.ANY),
                  pl.BlockSpec((tm, D_MODEL), lambda i: (i, 0)),
                  pl.BlockSpec((tm, TOP_K), lambda i: (i, 0)),
                  pl.BlockSpec((1, 1, D_MODEL), lambda i: (seg_of(i), 0, 0))],
        out_specs=pl.BlockSpec((tm, D_MODEL), lambda i: (i, 0)),
        out_shape=jax.ShapeDtypeStruct((n, D_MODEL), F32),
        scratch_shapes=[pltpu.VMEM((2, TOP_K * tm, ROW_SLABS, LANES), F32), pltpu.SemaphoreType.DMA((2,))],
        compiler_params=_params(),
        name="moe_combine",
    )(dest_tiles, dest_tiles, y_rows, x, gate, g2)


def moe_layer(x, g, sc, sh, g2, router_w, router_b, w_gu, b_gu, w_down, b_down, segs):
    n = x.shape[0]
    n_tiles = -(-(n * TOP_K + N_EXPERTS * (MOE_BLOCK - 1)) // MOE_BLOCK)
    h, top_idx_t, gate_t, rank_t, counts = norm_router(x, g, sc, sh, router_w, router_b, segs)
    dest, row_tok, tile_e, n_used = moe_routing(top_idx_t, rank_t, counts[:, 0].astype(jnp.int32), n_tiles)
    y_rows = moe_experts(h, row_tok, tile_e, n_used, w_gu, b_gu, w_down, b_down)
    return moe_combine(x, y_rows, dest, gate_t.T, g2, segs)


def _pad_cols(w, width):
    return jnp.pad(w, ((0, 0), (0, width - w.shape[1])))


def trunk(x, c, p, segs):
    nseg = segs.nseg
    rows = -(-nseg // 8) * 8
    c_pad = jnp.pad(c, ((0, rows - nseg), (0, 0)))
    mod = ada_modulation(c_pad, p['ada_w'], p['ada_b'])[:, :nseg].reshape(DEPTH, nseg, 6, 1, D_MODEL)

    for layer in range(DEPTH):
        sh1, sc1, g1, sh2, sc2, g2 = [mod[layer, :, j] for j in range(6)]
        i = layer // 2
        n1 = p['norm1_g'][layer].reshape(1, -1)
        if layer % 2 == 0:
            w_in = p['ev_w_in'][i].astype(BF16)
            w_a = w_in[:, :1536]
            w_lr = _pad_cols(w_in[:, 1536:1568], 128)
            w_c = w_in[:, 1568:]
            a_proj, lr_proj, c_proj = norm_proj(x, n1, sc1, sh1, [w_a, w_lr, w_c], segs)
            o_f, o_b = gla_bidir(a_proj, lr_proj, p['gla_wa_f'][i], p['gla_ba_f'][i],
                                 p['gla_wa_b'][i], p['gla_ba_b'][i], segs)
            x = even_out(x, o_f, o_b, a_proj, c_proj, p['gla_norm_g'][i], p['conv_w'][i],
                         p['ev_w_out'][i].astype(BF16), g1, segs)
        else:
            w_in = p['od_w_in'][i].astype(BF16)
            q_proj, kv_proj, du = norm_proj(x, n1, sc1, sh1, [w_in[:, :512], w_in[:, 512:768], w_in[:, 768:]], segs)
            att = windowed_attention(q_proj, kv_proj, p['q_norm_g'][i], p['k_norm_g'][i], p['attn_sink'][i], segs)
            s5_params = (p['s5_lam_re'][i], p['s5_lam_im'][i], p['s5_log_step'][i],
                         p['s5_b_re'][i], p['s5_b_im'][i], p['s5_c_re'][i], p['s5_c_im'][i])
            ys = s5_conv(du, s5_params, segs)
            x = odd_out(x, att, du, ys, p['s5_d'][i], p['s5_glu_w'][i].astype(BF16), p['s5_glu_b'][i],
                        p['od_w_out'][i].astype(BF16), g1, segs)
        x = moe_layer(x, p['norm2_g'][layer].reshape(1, -1), sc2, sh2, g2,
                      p['router_w'][layer], p['router_b'][layer],
                      p['exp_w_gu'][layer].astype(BF16), p['exp_b_gu'][layer].reshape(N_EXPERTS, 1, -1),
                      p['exp_w_down'][layer].astype(BF16), p['exp_b_down'][layer].reshape(N_EXPERTS, 1, -1), segs)
    return x


def kernel(x_prompt, x_sample, c_prompt, c_sample, ada_w, ada_b, norm1_g, norm2_g, ev_w_in, ev_w_out, gla_wa_f, gla_ba_f, gla_wa_b, gla_ba_b, gla_norm_g, conv_w, od_w_in, od_w_out, q_norm_g, k_norm_g, attn_sink, s5_lam_re, s5_lam_im, s5_log_step, s5_b_re, s5_b_im, s5_c_re, s5_c_im, s5_d, s5_glu_w, s5_glu_b, router_w, router_b, exp_w_gu, exp_b_gu, exp_w_down, exp_b_down):
    p = dict(ada_w=ada_w, ada_b=ada_b, norm1_g=norm1_g, norm2_g=norm2_g,
             ev_w_in=ev_w_in, ev_w_out=ev_w_out, gla_wa_f=gla_wa_f, gla_ba_f=gla_ba_f,
             gla_wa_b=gla_wa_b, gla_ba_b=gla_ba_b, gla_norm_g=gla_norm_g, conv_w=conv_w,
             od_w_in=od_w_in, od_w_out=od_w_out, q_norm_g=q_norm_g, k_norm_g=k_norm_g,
             attn_sink=attn_sink, s5_lam_re=s5_lam_re, s5_lam_im=s5_lam_im, s5_log_step=s5_log_step,
             s5_b_re=s5_b_re, s5_b_im=s5_b_im, s5_c_re=s5_c_re, s5_c_im=s5_c_im, s5_d=s5_d,
             s5_glu_w=s5_glu_w, s5_glu_b=s5_glu_b, router_w=router_w, router_b=router_b,
             exp_w_gu=exp_w_gu, exp_b_gu=exp_b_gu, exp_w_down=exp_w_down, exp_b_down=exp_b_down)
    bp, tp, _ = x_prompt.shape
    bs, ts, _ = x_sample.shape
    seg = math.gcd(tp, ts)
    per_p, per_s = tp // seg, ts // seg
    assert bs == 1 and per_p == 1 and per_s == 2, "segment layout: prompt sequences of one segment, one sample sequence of two"
    nseg = bp * per_p + bs * per_s
    segs = Segs(seg=seg, nseg=nseg, join=bp * per_p + 1)
    x = jnp.concatenate([x_prompt.reshape(-1, D_MODEL), x_sample.reshape(-1, D_MODEL)], axis=0)
    c = jnp.concatenate([c_prompt, jnp.repeat(c_sample, per_s, axis=0)], axis=0)
    y = trunk(x, c, p, segs)
    n_p = bp * tp
    return y[:n_p].reshape(x_prompt.shape), y[n_p:].reshape(x_sample.shape)
```

```python
import functools
import math
from typing import NamedTuple

import numpy as np
import jax
import jax.numpy as jnp
from jax import lax
from jax.experimental import pallas as pl
from jax.experimental.pallas import tpu as pltpu

F32 = jnp.float32
BF16 = jnp.bfloat16

D_MODEL = 1024
DEPTH = 4
HALF = 512
HEAD_DIM = 64

GLA_HEADS = 4
GLA_DV = 128
GLA_DK = 64
GLA_RANK = 16
GLA_TAU = 16.0
GLA_CHUNK = 64

CONV_WIDTH = 3

ATT_Q_HEADS = 8
ATT_KV_HEADS = 2
ATT_WINDOW = 128

S5_GROUP = 16
S5_GROUPS = 32
S5_STATE = 64
S5_CHUNK = 16

N_EXPERTS = 32
TOP_K = 4
SWIGLU_LIMIT = 7.0
SWIGLU_ALPHA = 1.702
MOE_BLOCK = 512

NORM_EPS = 1e-6
NEG_BIG = -1e30

LANES = 128
ROW_SLABS = D_MODEL // LANES
GATHER_UNROLL = 8

VMEM_LIMIT_BYTES = 52 * 1024 * 1024

NT_DIMS = (((1,), (1,)), ((), ()))
TN_DIMS = (((0,), (0,)), ((), ()))


class Segs(NamedTuple):
    seg: int
    nseg: int
    join: int


def _params(n_axes=1):
    return pltpu.CompilerParams(dimension_semantics=("arbitrary",) * n_axes,
                                vmem_limit_bytes=VMEM_LIMIT_BYTES)


def _split_bf16(a):
    hi = a.astype(BF16)
    lo = (a - hi.astype(F32)).astype(BF16)
    return hi, lo


def _dot(a, b):
    return jnp.dot(a, b, preferred_element_type=F32)


def _dot_split(a, b):
    a_hi, a_lo = _split_bf16(a)
    b_hi, b_lo = _split_bf16(b)
    return _dot(a_hi, b_hi) + _dot(a_lo, b_hi) + _dot(a_hi, b_lo)


def _sublane_transpose8(vs):
    axis = vs[0].ndim - 2
    sub = lax.broadcasted_iota(jnp.int32, vs[0].shape, axis)
    vs = list(vs)
    for d in (4, 2, 1):
        low = (sub & d) == 0
        out = list(vs)
        for i in range(8):
            if i & d == 0:
                out[i] = jnp.where(low, vs[i], pltpu.roll(vs[i + d], d, axis=axis))
                out[i + d] = jnp.where(low, pltpu.roll(vs[i], 8 - d, axis=axis), vs[i + d])
        vs = out
    return vs


def _slabs_to_rows(slabs):
    rows = slabs.shape[0]
    groups = slabs.reshape(rows // 8, 8, ROW_SLABS, LANES)
    cols = _sublane_transpose8([groups[:, i] for i in range(8)])
    return jnp.concatenate([c.reshape(rows, LANES) for c in cols], axis=1)


def _rows_to_slabs(value):
    rows = value.shape[0]
    cols = [value[:, s * LANES:(s + 1) * LANES].reshape(rows // 8, 8, LANES) for s in range(ROW_SLABS)]
    tiles = _sublane_transpose8(cols)
    return jnp.stack(tiles, axis=1).reshape(rows, ROW_SLABS, LANES)


def _rms_mod(x, g, sc, sh):
    ms = jnp.mean(x * x, axis=-1, keepdims=True)
    h = x * lax.rsqrt(ms + NORM_EPS) * g
    return h * (1.0 + sc) + sh


def _ada_kernel(c_ref, w_ref, b_ref, o_ref):
    c = c_ref[...]
    s = c * jax.nn.sigmoid(c)
    o_ref[0] = _dot_split(s, w_ref[0]) + b_ref[0]


def ada_modulation(c_pad, ada_w, ada_b):
    rows = c_pad.shape[0]
    cols = 6 * D_MODEL
    tn = 1536
    return pl.pallas_call(
        _ada_kernel,
        grid=(DEPTH, cols // tn),
        in_specs=[pl.BlockSpec((rows, D_MODEL), lambda l, j: (0, 0)),
                  pl.BlockSpec((1, D_MODEL, tn), lambda l, j: (l, 0, j)),
                  pl.BlockSpec((1, 1, tn), lambda l, j: (l, 0, j))],
        out_specs=pl.BlockSpec((1, rows, tn), lambda l, j: (l, 0, j)),
        out_shape=jax.ShapeDtypeStruct((DEPTH, rows, cols), F32),
        compiler_params=_params(2),
        name="ada_modulation",
    )(c_pad, ada_w, ada_b.reshape(DEPTH, 1, cols))


def _norm_proj_kernel(x_ref, g_ref, sc_ref, sh_ref, *refs, n_w):
    h = _rms_mod(x_ref[...], g_ref[...], sc_ref[0], sh_ref[0]).astype(BF16)
    for w_ref, o_ref in zip(refs[:n_w], refs[n_w:]):
        o_ref[...] = _dot(h, w_ref[...]).astype(o_ref.dtype)


def norm_proj(x, g, sc, sh, weights, segs, tm=512):
    n = x.shape[0]
    n_w = len(weights)
    seg_of = lambda i: (i * tm) // segs.seg
    in_specs = [pl.BlockSpec((tm, D_MODEL), lambda i: (i, 0)),
                pl.BlockSpec((1, D_MODEL), lambda i: (0, 0)),
                pl.BlockSpec((1, 1, D_MODEL), lambda i: (seg_of(i), 0, 0)),
                pl.BlockSpec((1, 1, D_MODEL), lambda i: (seg_of(i), 0, 0))]
    in_specs += [pl.BlockSpec(w.shape, lambda i: (0, 0)) for w in weights]
    return pl.pallas_call(
        functools.partial(_norm_proj_kernel, n_w=n_w),
        grid=(n // tm,),
        in_specs=in_specs,
        out_specs=[pl.BlockSpec((tm, w.shape[1]), lambda i: (i, 0)) for w in weights],
        out_shape=[jax.ShapeDtypeStruct((n, w.shape[1]), F32) for w in weights],
        compiler_params=_params(),
        name="norm_proj",
    )(x, g, sc, sh, *weights)


def _gla_kernel(qk_f, v_f, lr_f, qk_b, v_b, lr_b, wa_f, ba_f, wa_b, ba_b, cum_f, cum_b,
                of_ref, ob_ref, st_f, st_b, *, segs, rows):
    i = pl.program_id(0)
    nb = pl.num_programs(0)
    ib = nb - 1 - i
    bps = segs.seg // rows
    seg_f = i // bps
    seg_b = ib // bps

    @pl.when((i % bps == 0) & (seg_f != segs.join))
    def _():
        st_f[...] = jnp.zeros_like(st_f)

    @pl.when((ib % bps == bps - 1) & (seg_b + 1 != segs.join))
    def _():
        st_b[...] = jnp.zeros_like(st_b)

    L = GLA_CHUNK
    scale = GLA_DK ** -0.5
    n_chunks = rows // L
    dkw = GLA_HEADS * GLA_DK

    def direction(qk_ref, v_ref, lr_ref, wa_ref, ba_ref, cum_ref, st_ref, o_ref, reverse):
        qk = qk_ref[...]
        q = qk[:, :dkw] * scale
        k = qk[:, dkw:]
        v = v_ref[...].astype(BF16)
        pre = _dot_split(lr_ref[...], wa_ref[...]) + ba_ref[...]
        log_sig = jnp.minimum(pre, 0.0) - jnp.log(1.0 + jnp.exp(-jnp.abs(pre)))
        g_hi, g_lo = _split_bf16(log_sig * (1.0 / GLA_TAU))
        cum = cum_ref[...]
        b = _dot(cum, g_hi) + _dot(cum, g_lo)
        edge = 0 if reverse else L - 1
        b_edge = jnp.concatenate([jnp.broadcast_to(b[c * L + edge:c * L + edge + 1, :], (L, dkw))
                                  for c in range(n_chunks)], axis=0)
        qe = (q * jnp.exp(b)).astype(BF16)
        ke = (k * jnp.exp(-b)).astype(BF16)
        kd = (k * jnp.exp(b_edge - b)).astype(BF16)
        d_edge = jnp.exp(b_edge)
        mask = cum != 0
        order = range(n_chunks - 1, -1, -1) if reverse else range(n_chunks)
        outs = []
        for h in range(GLA_HEADS):
            ks = slice(h * GLA_DK, (h + 1) * GLA_DK)
            qeh, keh, kdh = qe[:, ks], ke[:, ks], kd[:, ks]
            vh = v[:, h * GLA_DV:(h + 1) * GLA_DV]
            att = lax.dot_general(qeh, keh, NT_DIMS, preferred_element_type=F32)
            o = _dot(jnp.where(mask, att, 0.0).astype(BF16), vh)
            st = st_ref[h]
            inter = [None] * n_chunks
            for c in order:
                rs = slice(c * L, (c + 1) * L)
                inter[c] = lax.dot_general(qeh[rs], st.astype(BF16), NT_DIMS, preferred_element_type=F32)
                st = st * d_edge[c * L:c * L + 1, ks] + lax.dot_general(vh[rs], kdh[rs], TN_DIMS,
                                                                        preferred_element_type=F32)
            st_ref[h] = st
            outs.append(o + jnp.concatenate(inter, axis=0))
        o_ref[...] = jnp.concatenate(outs, axis=1)

    direction(qk_f, v_f, lr_f, wa_f, ba_f, cum_f, st_f, of_ref, False)
    direction(qk_b, v_b, lr_b, wa_b, ba_b, cum_b, st_b, ob_ref, True)


def gla_bidir(a_proj, lr_proj, wa_f, ba_f, wa_b, ba_b, segs, rows=512):
    n = a_proj.shape[0]
    nb = n // rows
    lanes = lr_proj.shape[1]
    wa_f_pad = jnp.zeros((lanes, GLA_HEADS * GLA_DK), F32).at[:GLA_RANK].set(wa_f)
    wa_b_pad = jnp.zeros((lanes, GLA_HEADS * GLA_DK), F32).at[GLA_RANK:2 * GLA_RANK].set(wa_b)
    t_io = jnp.arange(rows, dtype=jnp.int32)
    same_chunk = (t_io[:, None] // GLA_CHUNK) == (t_io[None, :] // GLA_CHUNK)
    cum_f = (same_chunk & (t_io[None, :] <= t_io[:, None])).astype(BF16)
    cum_b = (same_chunk & (t_io[None, :] >= t_io[:, None])).astype(BF16)
    fwd = lambda i: (i, 0)
    bwd = lambda i: (nb - 1 - i, 0)
    const = lambda i: (0, 0)
    spec = lambda width, col, row_map: pl.BlockSpec((rows, width), lambda i: (row_map(i)[0], col))
    return pl.pallas_call(
        functools.partial(_gla_kernel, segs=segs, rows=rows),
        grid=(nb,),
        in_specs=[spec(512, 0, fwd), spec(512, 1, fwd), pl.BlockSpec((rows, lanes), fwd),
                  spec(512, 0, bwd), spec(512, 1, bwd), pl.BlockSpec((rows, lanes), bwd),
                  pl.BlockSpec(wa_f_pad.shape, const), pl.BlockSpec((1, 256), const),
                  pl.BlockSpec(wa_b_pad.shape, const), pl.BlockSpec((1, 256), const),
                  pl.BlockSpec((rows, rows), const), pl.BlockSpec((rows, rows), const)],
        out_specs=[pl.BlockSpec((rows, HALF), fwd), pl.BlockSpec((rows, HALF), bwd)],
        out_shape=[jax.ShapeDtypeStruct((n, HALF), F32)] * 2,
        scratch_shapes=[pltpu.VMEM((GLA_HEADS, GLA_DV, GLA_DK), F32)] * 2,
        compiler_params=_params(),
        name="gla_bidir",
    )(a_proj, a_proj, lr_proj, a_proj, a_proj, lr_proj,
      wa_f_pad, ba_f.reshape(1, -1), wa_b_pad, ba_b.reshape(1, -1), cum_f, cum_b)


def _even_out_kernel(x_ref, of_ref, ob_ref, og_ref, c_ref, cprev_ref, cnext_ref, ng_ref, cw_ref,
                     w_ref, g1_ref, o_ref, *, segs, tm):
    i = pl.program_id(0)
    tps = segs.seg // tm
    seg = i // tps
    has_prev = jnp.logical_not((i % tps == 0) & (seg != segs.join))
    has_next = jnp.logical_not((i % tps == tps - 1) & (seg + 1 != segs.join))

    o = of_ref[...] + ob_ref[...]
    og = og_ref[...]
    parts = []
    for h in range(GLA_HEADS):
        oh = o[:, h * GLA_DV:(h + 1) * GLA_DV]
        ms = jnp.mean(oh * oh, axis=-1, keepdims=True)
        parts.append(oh * lax.rsqrt(ms + NORM_EPS) * ng_ref[...])
    a_out = jnp.concatenate(parts, axis=1) * (og * jax.nn.sigmoid(og))

    c = c_ref[...]
    bg = c[:, :HALF]
    u = c[:, HALF:2 * HALF] * c[:, 2 * HALF:]
    cp = cprev_ref[7:8, :]
    cn = cnext_ref[0:1, :]
    u_prev_edge = jnp.where(has_prev, cp[:, HALF:2 * HALF] * cp[:, 2 * HALF:], 0.0)
    u_next_edge = jnp.where(has_next, cn[:, HALF:2 * HALF] * cn[:, 2 * HALF:], 0.0)
    row = lax.broadcasted_iota(jnp.int32, (tm, 1), 0)
    u_prev = jnp.where(row == 0, u_prev_edge, pltpu.roll(u, 1, axis=0))
    u_next = jnp.where(row == tm - 1, u_next_edge, pltpu.roll(u, tm - 1, axis=0))
    cw = cw_ref[...]
    b_out = bg * (cw[0:1, :] * u_prev + cw[1:2, :] * u + cw[2:3, :] * u_next)

    w = w_ref[...]
    mix = _dot(a_out.astype(BF16), w[:HALF, :]) + _dot(b_out.astype(BF16), w[HALF:, :])
    o_ref[...] = x_ref[...] + g1_ref[0] * mix


def even_out(x, o_f, o_b, a_proj, c_proj, norm_g, conv_w, w_out, g1, segs, tm=512):
    n = x.shape[0]
    last8 = n // 8 - 1
    seg_of = lambda i: (i * tm) // segs.seg
    return pl.pallas_call(
        functools.partial(_even_out_kernel, segs=segs, tm=tm),
        grid=(n // tm,),
        in_specs=[pl.BlockSpec((tm, D_MODEL), lambda i: (i, 0)),
                  pl.BlockSpec((tm, HALF), lambda i: (i, 0)),
                  pl.BlockSpec((tm, HALF), lambda i: (i, 0)),
                  pl.BlockSpec((tm, HALF), lambda i: (i, 2)),
                  pl.BlockSpec((tm, 3 * HALF), lambda i: (i, 0)),
                  pl.BlockSpec((8, 3 * HALF), lambda i: (jnp.maximum(i * (tm // 8) - 1, 0), 0)),
                  pl.BlockSpec((8, 3 * HALF), lambda i: (jnp.minimum((i + 1) * (tm // 8), last8), 0)),
                  pl.BlockSpec((1, GLA_DV), lambda i: (0, 0)),
                  pl.BlockSpec((CONV_WIDTH, HALF), lambda i: (0, 0)),
                  pl.BlockSpec((D_MODEL, D_MODEL), lambda i: (0, 0)),
                  pl.BlockSpec((1, 1, D_MODEL), lambda i: (seg_of(i), 0, 0))],
        out_specs=pl.BlockSpec((tm, D_MODEL), lambda i: (i, 0)),
        out_shape=jax.ShapeDtypeStruct((n, D_MODEL), F32),
        compiler_params=_params(),
        name="even_out",
    )(x, o_f, o_b, a_proj, c_proj, c_proj, c_proj, norm_g.reshape(1, -1), conv_w, w_out, g1)


def _attn_kernel(q_ref, kv_ref, kvp_ref, kvn_ref, qg_ref, kg_ref, sink_ref, o_ref, *, segs, tq):
    i = pl.program_id(0)
    W = ATT_WINDOW
    tps = segs.seg // tq
    seg = i // tps
    first = (i % tps == 0) & (seg != segs.join)
    last = (i % tps == tps - 1) & (seg + 1 != segs.join)

    kv_all = jnp.concatenate([kvp_ref[...], kv_ref[...], kvn_ref[...]], axis=0)
    kvw = ATT_KV_HEADS * HEAD_DIM
    k_heads, v_heads = [], []
    for h in range(ATT_KV_HEADS):
        kh = kv_all[:, h * HEAD_DIM:(h + 1) * HEAD_DIM]
        ms = jnp.mean(kh * kh, axis=-1, keepdims=True)
        k_heads.append((kh * lax.rsqrt(ms + NORM_EPS) * kg_ref[...]).astype(BF16))
        v_heads.append(kv_all[:, kvw + h * HEAD_DIM:kvw + (h + 1) * HEAD_DIM].astype(BF16))

    t_io = lax.broadcasted_iota(jnp.int32, (W, 3 * W), 0)
    j_io = lax.broadcasted_iota(jnp.int32, (W, 3 * W), 1)
    rel = j_io - W - t_io
    dist = jnp.abs(rel)
    in_window = dist <= W
    dist_f = dist.astype(F32)
    group = ATT_Q_HEADS // ATT_KV_HEADS
    n_blk = tq // W
    for blk in range(n_blk):
        valid = in_window
        if blk == 0:
            valid = valid & ((j_io >= W) | jnp.logical_not(first))
        if blk == n_blk - 1:
            valid = valid & ((j_io < 2 * W) | jnp.logical_not(last))
        q = q_ref[blk * W:(blk + 1) * W, :]
        outs = []
        for hq in range(ATT_Q_HEADS):
            kvh = hq // group
            qh = q[:, hq * HEAD_DIM:(hq + 1) * HEAD_DIM]
            ms = jnp.mean(qh * qh, axis=-1, keepdims=True)
            qn = (qh * lax.rsqrt(ms + NORM_EPS) * qg_ref[...] * (HEAD_DIM ** -0.5)).astype(BF16)
            kh = k_heads[kvh][blk * W:(blk + 3) * W, :]
            vh = v_heads[kvh][blk * W:(blk + 3) * W, :]
            s = lax.dot_general(qn, kh, NT_DIMS, preferred_element_type=F32)
            slope = 2.0 ** (-8.0 * (hq + 1) / ATT_Q_HEADS)
            s = jnp.where(valid, s - slope * dist_f, NEG_BIG)
            sk = sink_ref[hq]
            m = jnp.maximum(jnp.max(s, axis=-1, keepdims=True), sk)
            p = jnp.exp(s - m)
            denom = jnp.sum(p, axis=-1, keepdims=True) + jnp.exp(sk - m)
            outs.append(_dot(p.astype(BF16), vh) / denom)
        o_ref[blk * W:(blk + 1) * W, :] = jnp.concatenate(outs, axis=1)


def windowed_attention(q_proj, kv_proj, q_norm_g, k_norm_g, sink, segs, tq=512):
    n = q_proj.shape[0]
    W = ATT_WINDOW
    r = tq // W
    last = n // W - 1
    kvw = 2 * ATT_KV_HEADS * HEAD_DIM
    return pl.pallas_call(
        functools.partial(_attn_kernel, segs=segs, tq=tq),
        grid=(n // tq,),
        in_specs=[pl.BlockSpec((tq, HALF), lambda i: (i, 0)),
                  pl.BlockSpec((tq, kvw), lambda i: (i, 0)),
                  pl.BlockSpec((W, kvw), lambda i: (jnp.maximum(i * r - 1, 0), 0)),
                  pl.BlockSpec((W, kvw), lambda i: (jnp.minimum((i + 1) * r, last), 0)),
                  pl.BlockSpec((1, HEAD_DIM), lambda i: (0, 0)),
                  pl.BlockSpec((1, HEAD_DIM), lambda i: (0, 0)),
                  pl.BlockSpec(memory_space=pltpu.SMEM)],
        out_specs=pl.BlockSpec((tq, HALF), lambda i: (i, 0)),
        out_shape=jax.ShapeDtypeStruct((n, HALF), F32),
        compiler_params=_params(),
        name="windowed_attention",
    )(q_proj, kv_proj, kv_proj, kv_proj, q_norm_g.reshape(1, -1), k_norm_g.reshape(1, -1), sink)


def s5_tables(lam_re, lam_im, log_step, b_re, b_im, c_re, c_im, *, n_inner):
    L = S5_CHUNK
    dt = jnp.exp(log_step)[:, :, None]
    lr, li = lam_re, lam_im
    mag = jnp.exp(lr * dt)
    ar, ai = mag * jnp.cos(li * dt), mag * jnp.sin(li * dt)
    den = lr * lr + li * li
    zr = ((ar - 1.0) * lr + ai * li) / den
    zi = (ai * lr - (ar - 1.0) * li) / den
    bbr = zr[..., None] * b_re - zi[..., None] * b_im
    bbi = zr[..., None] * b_im + zi[..., None] * b_re
    tau = jnp.arange(L + 1, dtype=F32)[:, None, None, None]
    pmag = jnp.exp(lr[None] * dt[None] * tau)
    pang = li[None] * dt[None] * tau
    pr, pi = pmag * jnp.cos(pang), pmag * jnp.sin(pang)

    hp = lax.Precision.HIGHEST
    car = c_re[None] * pr[:, :, :, None, :] - c_im[None] * pi[:, :, :, None, :]
    cai = c_re[None] * pi[:, :, :, None, :] + c_im[None] * pr[:, :, :, None, :]
    kern = (jnp.einsum('ldgop,dgpi->ldgoi', car, bbr, precision=hp)
            - jnp.einsum('ldgop,dgpi->ldgoi', cai, bbi, precision=hp))
    s_idx = jnp.arange(L)[:, None]
    t_idx = jnp.arange(L)[None, :]
    lag_f = jnp.clip(t_idx - s_idx, 0, L)
    lag_b = jnp.clip(s_idx - t_idx, 0, L)
    kf = jnp.where((t_idx >= s_idx)[:, :, None, None, None], kern[:, 0][lag_f], 0.0)
    kb = jnp.where((s_idx >= t_idx)[:, :, None, None, None], kern[:, 1][lag_b], 0.0)
    m = (kf + kb).transpose(2, 0, 4, 1, 3).reshape(S5_GROUPS, L * S5_GROUP, L * S5_GROUP)

    abr = pr[..., None] * bbr[None] - pi[..., None] * bbi[None]
    abi = pr[..., None] * bbi[None] + pi[..., None] * bbr[None]
    e_f = L - 1 - jnp.arange(L)
    e_b = jnp.arange(L)
    to_rows = lambda a: a.transpose(1, 0, 3, 2).reshape(S5_GROUPS, L * S5_GROUP, S5_STATE)
    wp = jnp.concatenate([to_rows(abr[e_f, 0]), to_rows(abi[e_f, 0]),
                          to_rows(abr[e_b, 1]), to_rows(abi[e_b, 1])], axis=-1)

    o_f = jnp.arange(L) + 1
    o_b = L - jnp.arange(L)
    to_cols = lambda a: a.transpose(1, 3, 0, 2).reshape(S5_GROUPS, S5_STATE, L * S5_GROUP)
    wc = jnp.concatenate([to_cols(car[o_f, 0]), -to_cols(cai[o_f, 0]),
                          to_cols(car[o_b, 1]), -to_cols(cai[o_b, 1])], axis=1)

    steps = jnp.arange(n_inner + 1, dtype=F32)[:, None, None, None] * float(L)
    qmag = jnp.exp(lr[None] * dt[None] * steps)
    qang = li[None] * dt[None] * steps
    qr, qi = qmag * jnp.cos(qang), qmag * jnp.sin(qang)
    form_a = jnp.concatenate([qr, qr], -1)
    form_b = jnp.concatenate([-qi, qi], -1)
    dpow = jnp.stack([form_a[:, 0], form_b[:, 0], form_a[:, 1], form_b[:, 1]], axis=0)
    dpow = dpow.transpose(2, 0, 1, 3)
    return m.astype(BF16), wp.astype(BF16), wc, dpow


def _s5_kernel(x_ref, m_ref, wp_ref, wc_ref, dpow_ref, y_ref, pf_scr, pb_scr, sf_scr, sb_scr, *, segs, n_sub):
    P2 = 2 * S5_STATE
    n_inner = segs.seg // S5_CHUNK // n_sub
    rows_step = segs.nseg * n_sub
    x = x_ref[0]
    y_ref[0] = _dot(x, m_ref[0])
    p = _dot(x, wp_ref[0])
    pf_scr[...] = p[:, :P2]
    pb_scr[...] = p[:, P2:]

    def cmul(s, s_swapped, form_a, form_b):
        return s * form_a + s_swapped * form_b

    def swap(s):
        return pltpu.roll(s, S5_STATE, axis=1)

    def power(form, i):
        return dpow_ref[0, form, pl.ds(i, 1), :]

    d1 = [power(f, 1) for f in range(4)]
    dn = [power(f, n_inner) for f in range(4)]

    def block(i):
        return pl.ds(pl.multiple_of(i * rows_step, 8), rows_step)

    def local_step(i, carry):
        s_f, s_b = carry
        rf, rb = block(i), block(n_inner - 1 - i)
        sf_scr[rf, :] = s_f
        sb_scr[rb, :] = s_b
        return (cmul(s_f, swap(s_f), d1[0], d1[1]) + pf_scr[rf, :],
                cmul(s_b, swap(s_b), d1[2], d1[3]) + pb_scr[rb, :])
    zeros = jnp.zeros((rows_step, P2), F32)
    end_f, end_b = lax.fori_loop(0, n_inner, local_step, (zeros, zeros))

    r_io = lax.broadcasted_iota(jnp.int32, (rows_step, 1), 0)
    j_io, seg_io = r_io % n_sub, r_io // n_sub
    takes_prev = jnp.logical_not((j_io == 0) & (seg_io != segs.join))
    takes_next = jnp.logical_not((j_io == n_sub - 1) & (seg_io + 1 != segs.join))
    end_f_prev = pltpu.roll(end_f, 1, axis=0)
    end_b_next = pltpu.roll(end_b, rows_step - 1, axis=0)
    c_f, c_b = zeros, zeros
    longest = n_sub * (2 if segs.join >= 0 else 1)
    for _ in range(longest - 1):
        c_prev = pltpu.roll(c_f, 1, axis=0)
        c_f = jnp.where(takes_prev, cmul(c_prev, swap(c_prev), dn[0], dn[1]) + end_f_prev, 0.0)
        c_next = pltpu.roll(c_b, rows_step - 1, axis=0)
        c_b = jnp.where(takes_next, cmul(c_next, swap(c_next), dn[2], dn[3]) + end_b_next, 0.0)

    c_f_sw, c_b_sw = swap(c_f), swap(c_b)

    def fix_step(i, carry):
        rf, rb = block(i), block(n_inner - 1 - i)
        sf_scr[rf, :] += cmul(c_f, c_f_sw, power(0, i), power(1, i))
        sb_scr[rb, :] += cmul(c_b, c_b_sw, power(2, i), power(3, i))
        return carry
    lax.fori_loop(0, n_inner, fix_step, 0)

    wc_hi, wc_lo = _split_bf16(wc_ref[0])
    acc = y_ref[0]
    for s_scr, rows in ((sf_scr, slice(0, P2)), (sb_scr, slice(P2, 2 * P2))):
        s_hi, s_lo = _split_bf16(s_scr[...])
        acc = acc + _dot(s_hi, wc_hi[rows]) + _dot(s_lo, wc_hi[rows]) + _dot(s_hi, wc_lo[rows])
    y_ref[0] = acc


def s5_conv(du, params, segs, n_sub=8):
    n = du.shape[0]
    L = S5_CHUNK
    rows = n // L
    width = L * S5_GROUP
    n_inner = segs.seg // L // n_sub
    m, wp, wc, dpow = s5_tables(*params, n_inner=n_inner)
    xg = du.astype(BF16).reshape(segs.nseg, n_sub, n_inner, L, S5_GROUPS, S5_GROUP)
    xg = xg.transpose(4, 2, 0, 1, 3, 5).reshape(S5_GROUPS, rows, width)
    grp = lambda g: (g, 0, 0)
    yg = pl.pallas_call(
        functools.partial(_s5_kernel, segs=segs, n_sub=n_sub),
        grid=(S5_GROUPS,),
        in_specs=[pl.BlockSpec((1, rows, width), grp),
                  pl.BlockSpec((1, width, width), grp),
                  pl.BlockSpec((1, width, 4 * S5_STATE), grp),
                  pl.BlockSpec((1, 4 * S5_STATE, width), grp),
                  pl.BlockSpec((1, 4, n_inner + 1, 2 * S5_STATE), lambda g: (g, 0, 0, 0))],
        out_specs=pl.BlockSpec((1, rows, width), grp),
        out_shape=jax.ShapeDtypeStruct((S5_GROUPS, rows, width), F32),
        scratch_shapes=[pltpu.VMEM((rows, 2 * S5_STATE), F32)] * 4,
        compiler_params=_params(),
        name="s5_conv",
    )(xg, m, wp, wc, dpow)
    yg = yg.reshape(S5_GROUPS, n_inner, segs.nseg, n_sub, L, S5_GROUP)
    return yg.transpose(2, 3, 1, 4, 0, 5).reshape(n, HALF)


def _odd_out_kernel(x_ref, att_ref, du_ref, ys_ref, dsk_ref, gw_ref, gb_ref, w_ref, g1_ref, o_ref):
    y = dsk_ref[...] * du_ref[...] + ys_ref[...]
    z = 0.5 * y * (1.0 + jnp.tanh(math.sqrt(2.0 / math.pi) * (y + 0.044715 * (y * y * y))))
    gate = jax.nn.sigmoid(_dot(z.astype(BF16), gw_ref[...]) + gb_ref[...])
    d_out = z * gate
    w = w_ref[...]
    mix = _dot(att_ref[...].astype(BF16), w[:HALF, :]) + _dot(d_out.astype(BF16), w[HALF:, :])
    o_ref[...] = x_ref[...] + g1_ref[0] * mix


def odd_out(x, att, du, ys, d_skip, glu_w, glu_b, w_out, g1, segs, tm=512):
    n = x.shape[0]
    seg_of = lambda i: (i * tm) // segs.seg
    row = lambda i: (i, 0)
    const = lambda i: (0, 0)
    return pl.pallas_call(
        _odd_out_kernel,
        grid=(n // tm,),
        in_specs=[pl.BlockSpec((tm, D_MODEL), row), pl.BlockSpec((tm, HALF), row),
                  pl.BlockSpec((tm, HALF), row), pl.BlockSpec((tm, HALF), row),
                  pl.BlockSpec((1, HALF), const), pl.BlockSpec((HALF, HALF), const),
                  pl.BlockSpec((1, HALF), const), pl.BlockSpec((D_MODEL, D_MODEL), const),
                  pl.BlockSpec((1, 1, D_MODEL), lambda i: (seg_of(i), 0, 0))],
        out_specs=pl.BlockSpec((tm, D_MODEL), row),
        out_shape=jax.ShapeDtypeStruct((n, D_MODEL), F32),
        compiler_params=_params(),
        name="odd_out",
    )(x, att, du, ys, d_skip.reshape(1, -1), glu_w, glu_b.reshape(1, -1), w_out, g1)


def _norm_router_kernel(x_ref, g_ref, sc_ref, sh_ref, wr_ref, br_ref, before_ref,
                        h_ref, idx_ref, gate_ref, rank_ref, count_ref):
    @pl.when(pl.program_id(0) == 0)
    def _():
        count_ref[...] = jnp.zeros_like(count_ref)

    h = _rms_mod(x_ref[...], g_ref[...], sc_ref[0], sh_ref[0])
    h_ref[...] = _rows_to_slabs(h)
    h_hi, h_lo = _split_bf16(h)
    w_hi, w_lo = _split_bf16(wr_ref[...])
    nt = lambda a, b: lax.dot_general(a, b, NT_DIMS, preferred_element_type=F32)
    logits = nt(w_hi, h_hi) + nt(w_lo, h_hi) + nt(w_hi, h_lo) + br_ref[...]
    e_io = lax.broadcasted_iota(jnp.int32, logits.shape, 0)
    tops, picks = [], []
    for k in range(TOP_K):
        m = jnp.max(logits, axis=0, keepdims=True)
        idx = jnp.min(jnp.where(logits == m, e_io, N_EXPERTS), axis=0, keepdims=True)
        idx_ref[k:k + 1, :] = idx
        pick = e_io == idx
        logits = jnp.where(pick, -jnp.inf, logits)
        tops.append(m)
        picks.append(pick)
    es = [jnp.exp(t - tops[0]) for t in tops]
    total = es[0] + es[1] + es[2] + es[3]
    for k in range(TOP_K):
        gate_ref[k:k + 1, :] = es[k] / total

    chosen = (picks[0] | picks[1] | picks[2] | picks[3]).astype(F32)
    rank = _dot(chosen.astype(BF16), before_ref[...]) + count_ref[:, 0:1]
    for k in range(TOP_K):
        rank_ref[k:k + 1, :] = jnp.sum(jnp.where(picks[k], rank, 0.0), axis=0, keepdims=True).astype(jnp.int32)
    count_ref[...] = count_ref[...] + jnp.sum(chosen, axis=1, keepdims=True)


def norm_router(x, g, sc, sh, router_w, router_b, segs, tm=512):
    n = x.shape[0]
    seg_of = lambda i: (i * tm) // segs.seg
    t_io = jnp.arange(tm, dtype=jnp.int32)
    before = (t_io[:, None] < t_io[None, :]).astype(BF16)
    return pl.pallas_call(
        _norm_router_kernel,
        grid=(n // tm,),
        in_specs=[pl.BlockSpec((tm, D_MODEL), lambda i: (i, 0)),
                  pl.BlockSpec((1, D_MODEL), lambda i: (0, 0)),
                  pl.BlockSpec((1, 1, D_MODEL), lambda i: (seg_of(i), 0, 0)),
                  pl.BlockSpec((1, 1, D_MODEL), lambda i: (seg_of(i), 0, 0)),
                  pl.BlockSpec((N_EXPERTS, D_MODEL), lambda i: (0, 0)),
                  pl.BlockSpec((N_EXPERTS, 1), lambda i: (0, 0)),
                  pl.BlockSpec((tm, tm), lambda i: (0, 0))],
        out_specs=[pl.BlockSpec((tm, ROW_SLABS, LANES), lambda i: (i, 0, 0)),
                   pl.BlockSpec((TOP_K, tm), lambda i: (0, i)),
                   pl.BlockSpec((TOP_K, tm), lambda i: (0, i)),
                   pl.BlockSpec((TOP_K, tm), lambda i: (0, i)),
                   pl.BlockSpec((N_EXPERTS, LANES), lambda i: (0, 0))],
        out_shape=[jax.ShapeDtypeStruct((n, ROW_SLABS, LANES), F32),
                   jax.ShapeDtypeStruct((TOP_K, n), jnp.int32),
                   jax.ShapeDtypeStruct((TOP_K, n), F32),
                   jax.ShapeDtypeStruct((TOP_K, n), jnp.int32),
                   jax.ShapeDtypeStruct((N_EXPERTS, LANES), F32)],
        compiler_params=_params(),
        name="norm_router",
    )(x, g, sc, sh, router_w.T, router_b.reshape(-1, 1), before)


def moe_routing(top_idx_t, rank_t, counts, n_tiles):
    n = top_idx_t.shape[1]
    padded = (counts + MOE_BLOCK - 1) // MOE_BLOCK * MOE_BLOCK
    padded_end = jnp.cumsum(padded)
    padded_start = padded_end - padded
    experts = jnp.arange(N_EXPERTS, dtype=jnp.int32)
    start_of = jnp.sum(jnp.where(top_idx_t[:, :, None] == experts, padded_start, 0), axis=-1)
    dest = start_of + rank_t
    n_used = (padded_end[-1] // MOE_BLOCK).astype(jnp.int32)
    tile_start = jnp.arange(n_tiles, dtype=jnp.int32) * MOE_BLOCK
    tile_e = jnp.sum((padded_end[None, :] <= tile_start[:, None]).astype(jnp.int32), axis=1)
    tile_e = jnp.minimum(tile_e, N_EXPERTS - 1)
    last_e = tile_e[jnp.maximum(n_used - 1, 0)]
    tile_e = jnp.where(jnp.arange(n_tiles) < n_used, tile_e, last_e)
    n_rows = jnp.full((1,), n_tiles * MOE_BLOCK, jnp.int32)
    pad_rows = jnp.stack([jnp.concatenate([padded_start + counts, padded_end[-1:]]),
                          jnp.concatenate([padded_end, n_rows])])
    return dest, pad_rows, tile_e, n_used.reshape(1)


def _dispatch_kernel(pad_ref, dest_ref, h_ref, rows_hbm, stage, zero_row, sems, pad_sem, *, tm):
    i = pl.program_id(0)
    last = pl.num_programs(0) - 1
    slot = i % 2

    def wait_slot(s):
        for _ in range(TOP_K):
            pltpu.make_async_copy(stage.at[s], rows_hbm.at[pl.ds(0, tm)], sems.at[s]).wait()

    @pl.when(i == 0)
    def _():
        zero_row[...] = jnp.zeros_like(zero_row)

        def per_expert(e, carry):
            lo, hi = pad_ref[0, e], pad_ref[1, e]

            def start(r, c):
                pltpu.make_async_copy(zero_row, rows_hbm.at[pl.ds(r, 1)], pad_sem).start()
                return c

            def wait(r, c):
                pltpu.make_async_copy(zero_row, rows_hbm.at[pl.ds(r, 1)], pad_sem).wait()
                return c
            lax.fori_loop(lo, hi, start, 0)
            lax.fori_loop(lo, hi, wait, 0)
            return carry
        lax.fori_loop(0, N_EXPERTS + 1, per_expert, 0)

    @pl.when(i >= 2)
    def _():
        wait_slot(slot)

    stage[slot] = h_ref[...]

    def issue(group, carry):
        for u in range(GATHER_UNROLL):
            j = group * GATHER_UNROLL + u
            row = j & (tm - 1)
            copy = pltpu.make_async_copy(stage.at[slot, pl.ds(row, 1)],
                                         rows_hbm.at[pl.ds(dest_ref[0, 0, j], 1)], sems.at[slot])
            copy.start(priority=u % 2)
        return carry
    lax.fori_loop(0, TOP_K * tm // GATHER_UNROLL, issue, 0)

    @pl.when(i == last)
    def _():
        @pl.when(i >= 1)
        def _():
            wait_slot(1 - slot)
        wait_slot(slot)


def moe_dispatch(h, dest, pad_rows, n_rows, tm=256):
    n = h.shape[0]
    nt = n // tm
    dest_tiles = dest.reshape(TOP_K, nt, tm).transpose(1, 0, 2).reshape(nt, 1, TOP_K * tm)
    grid_spec = pltpu.PrefetchScalarGridSpec(
        num_scalar_prefetch=1,
        grid=(nt,),
        in_specs=[pl.BlockSpec((1, 1, TOP_K * tm), lambda i, pad: (i, 0, 0), memory_space=pltpu.SMEM),
                  pl.BlockSpec((tm, ROW_SLABS, LANES), lambda i, pad: (i, 0, 0))],
        out_specs=pl.BlockSpec(memory_space=pl.ANY),
        scratch_shapes=[pltpu.VMEM((2, tm, ROW_SLABS, LANES), F32), pltpu.VMEM((1, ROW_SLABS, LANES), F32),
                        pltpu.SemaphoreType.DMA((2,)), pltpu.SemaphoreType.DMA(())],
    )
    return pl.pallas_call(
        functools.partial(_dispatch_kernel, tm=tm),
        grid_spec=grid_spec,
        out_shape=jax.ShapeDtypeStruct((n_rows, ROW_SLABS, LANES), F32),
        compiler_params=_params(),
        name="moe_dispatch",
    )(pad_rows, dest_tiles, h)


def _start_row_gather(src_hbm, dst_vmem, sem, idx_ref, n_rows, n_priorities=1):
    def issue(group, carry):
        for u in range(GATHER_UNROLL):
            r = group * GATHER_UNROLL + u
            copy = pltpu.make_async_copy(src_hbm.at[pl.ds(idx_ref[0, 0, r], 1)], dst_vmem.at[pl.ds(r, 1)], sem)
            copy.start(priority=u % n_priorities)
        return carry
    lax.fori_loop(0, n_rows // GATHER_UNROLL, issue, 0)


def _wait_row_gather(src_hbm, dst_vmem, sem, n_rows):
    pltpu.make_async_copy(src_hbm.at[pl.ds(0, n_rows)], dst_vmem, sem).wait()


def _expert_kernel(tile_e_ref, n_used_ref, x_ref, wgu_ref, bgu_ref, wd_ref, bd_ref, y_ref):
    t = pl.program_id(0)
    n_used = n_used_ref[0]

    @pl.when(t < n_used)
    def _():
        x = _slabs_to_rows(x_ref[...]).astype(BF16)
        gu = _dot(x, wgu_ref[0]) + bgu_ref[0]
        d_ff = gu.shape[1] // 2
        g_ = jnp.minimum(gu[:, :d_ff], SWIGLU_LIMIT)
        u_ = jnp.clip(gu[:, d_ff:], -SWIGLU_LIMIT, SWIGLU_LIMIT)
        hh = g_ * jax.nn.sigmoid(SWIGLU_ALPHA * g_) * (u_ + 1.0)
        y_ref[...] = _rows_to_slabs(_dot(hh.astype(BF16), wd_ref[0]) + bd_ref[0])

    @pl.when(t >= n_used)
    def _():
        y_ref[...] = jnp.zeros_like(y_ref)


def moe_experts(x_rows, tile_e, n_used, w_gu, b_gu, w_down, b_down):
    n_tiles = tile_e.shape[0]
    d_ff2 = w_gu.shape[2]
    grid_spec = pltpu.PrefetchScalarGridSpec(
        num_scalar_prefetch=2,
        grid=(n_tiles,),
        in_specs=[pl.BlockSpec((MOE_BLOCK, ROW_SLABS, LANES), lambda t, te, nu: (t, 0, 0)),
                  pl.BlockSpec((1, D_MODEL, d_ff2), lambda t, te, nu: (te[t], 0, 0)),
                  pl.BlockSpec((1, 1, d_ff2), lambda t, te, nu: (te[t], 0, 0)),
                  pl.BlockSpec((1, d_ff2 // 2, D_MODEL), lambda t, te, nu: (te[t], 0, 0)),
                  pl.BlockSpec((1, 1, D_MODEL), lambda t, te, nu: (te[t], 0, 0))],
        out_specs=pl.BlockSpec((MOE_BLOCK, ROW_SLABS, LANES), lambda t, te, nu: (t, 0, 0)),
    )
    return pl.pallas_call(
        _expert_kernel,
        grid_spec=grid_spec,
        out_shape=jax.ShapeDtypeStruct((n_tiles * MOE_BLOCK, ROW_SLABS, LANES), F32),
        compiler_params=_params(),
        name="moe_experts",
    )(tile_e, n_used, x_rows, w_gu, b_gu, w_down, b_down)


def _combine_kernel(dest_ref, dest_next_ref, y_hbm, x_ref, gate_ref, g2_ref, o_ref, ybuf, sems, *, tm):
    i = pl.program_id(0)
    slot = i % 2
    n_rows = TOP_K * tm

    @pl.when(i == 0)
    def _():
        _start_row_gather(y_hbm, ybuf.at[0], sems.at[0], dest_ref, n_rows, n_priorities=2)

    @pl.when(i + 1 < pl.num_programs(0))
    def _():
        _start_row_gather(y_hbm, ybuf.at[1 - slot], sems.at[1 - slot], dest_next_ref, n_rows, n_priorities=2)

    _wait_row_gather(y_hbm, ybuf.at[slot], sems.at[slot], n_rows)
    gate = gate_ref[...]
    acc = gate[:, 0:1] * _slabs_to_rows(ybuf[slot, 0:tm])
    for k in range(1, TOP_K):
        acc = acc + gate[:, k:k + 1] * _slabs_to_rows(ybuf[slot, k * tm:(k + 1) * tm])
    o_ref[...] = x_ref[...] + g2_ref[0] * acc


def moe_combine(x, y_rows, dest, gate, g2, segs, tm=256):
    n = x.shape[0]
    nt = n // tm
    dest_tiles = dest.reshape(TOP_K, nt, tm).transpose(1, 0, 2).reshape(nt, 1, TOP_K * tm)
    seg_of = lambda i: (i * tm) // segs.seg
    return pl.pallas_call(
        functools.partial(_combine_kernel, tm=tm),
        grid=(nt,),
        in_specs=[pl.BlockSpec((1, 1, TOP_K * tm), lambda i: (i, 0, 0), memory_space=pltpu.SMEM),
                  pl.BlockSpec((1, 1, TOP_K * tm), lambda i: (jnp.minimum(i + 1, nt - 1), 0, 0),
                               memory_space=pltpu.SMEM),
                  pl.BlockSpec(memory_space=pl.ANY),
                  pl.BlockSpec((tm, D_MODEL), lambda i: (i, 0)),
                  pl.BlockSpec((tm, TOP_K), lambda i: (i, 0)),
                  pl.BlockSpec((1, 1, D_MODEL), lambda i: (seg_of(i), 0, 0))],
        out_specs=pl.BlockSpec((tm, D_MODEL), lambda i: (i, 0)),
        out_shape=jax.ShapeDtypeStruct((n, D_MODEL), F32),
        scratch_shapes=[pltpu.VMEM((2, TOP_K * tm, ROW_SLABS, LANES), F32), pltpu.SemaphoreType.DMA((2,))],
        compiler_params=_params(),
        name="moe_combine",
    )(dest_tiles, dest_tiles, y_rows, x, gate, g2)


def moe_layer(x, g, sc, sh, g2, router_w, router_b, w_gu, b_gu, w_down, b_down, segs):
    n = x.shape[0]
    n_tiles = -(-(n * TOP_K + N_EXPERTS * (MOE_BLOCK - 1)) // MOE_BLOCK)
    h, top_idx_t, gate_t, rank_t, counts = norm_router(x, g, sc, sh, router_w, router_b, segs)
    dest, pad_rows, tile_e, n_used = moe_routing(top_idx_t, rank_t, counts[:, 0].astype(jnp.int32), n_tiles)
    x_rows = moe_dispatch(h, dest, pad_rows, n_tiles * MOE_BLOCK)
    y_rows = moe_experts(x_rows, tile_e, n_used, w_gu, b_gu, w_down, b_down)
    return moe_combine(x, y_rows, dest, gate_t.T, g2, segs)


def _pad_cols(w, width):
    return jnp.pad(w, ((0, 0), (0, width - w.shape[1])))


def trunk(x, c, p, segs):
    nseg = segs.nseg
    rows = -(-nseg // 8) * 8
    c_pad = jnp.pad(c, ((0, rows - nseg), (0, 0)))
    mod = ada_modulation(c_pad, p['ada_w'], p['ada_b'])[:, :nseg].reshape(DEPTH, nseg, 6, 1, D_MODEL)

    for layer in range(DEPTH):
        sh1, sc1, g1, sh2, sc2, g2 = [mod[layer, :, j] for j in range(6)]
        i = layer // 2
        n1 = p['norm1_g'][layer].reshape(1, -1)
        if layer % 2 == 0:
            w_in = p['ev_w_in'][i].astype(BF16)
            w_a = w_in[:, :1536]
            w_lr = _pad_cols(w_in[:, 1536:1568], 128)
            w_c = w_in[:, 1568:]
            a_proj, lr_proj, c_proj = norm_proj(x, n1, sc1, sh1, [w_a, w_lr, w_c], segs)
            o_f, o_b = gla_bidir(a_proj, lr_proj, p['gla_wa_f'][i], p['gla_ba_f'][i],
                                 p['gla_wa_b'][i], p['gla_ba_b'][i], segs)
            x = even_out(x, o_f, o_b, a_proj, c_proj, p['gla_norm_g'][i], p['conv_w'][i],
                         p['ev_w_out'][i].astype(BF16), g1, segs)
        else:
            w_in = p['od_w_in'][i].astype(BF16)
            q_proj, kv_proj, du = norm_proj(x, n1, sc1, sh1, [w_in[:, :512], w_in[:, 512:768], w_in[:, 768:]], segs)
            att = windowed_attention(q_proj, kv_proj, p['q_norm_g'][i], p['k_norm_g'][i], p['attn_sink'][i], segs)
            s5_params = (p['s5_lam_re'][i], p['s5_lam_im'][i], p['s5_log_step'][i],
                         p['s5_b_re'][i], p['s5_b_im'][i], p['s5_c_re'][i], p['s5_c_im'][i])
            ys = s5_conv(du, s5_params, segs)
            x = odd_out(x, att, du, ys, p['s5_d'][i], p['s5_glu_w'][i].astype(BF16), p['s5_glu_b'][i],
                        p['od_w_out'][i].astype(BF16), g1, segs)
        x = moe_layer(x, p['norm2_g'][layer].reshape(1, -1), sc2, sh2, g2,
                      p['router_w'][layer], p['router_b'][layer],
                      p['exp_w_gu'][layer].astype(BF16), p['exp_b_gu'][layer].reshape(N_EXPERTS, 1, -1),
                      p['exp_w_down'][layer].astype(BF16), p['exp_b_down'][layer].reshape(N_EXPERTS, 1, -1), segs)
    return x


def kernel(x_prompt, x_sample, c_prompt, c_sample, ada_w, ada_b, norm1_g, norm2_g, ev_w_in, ev_w_out, gla_wa_f, gla_ba_f, gla_wa_b, gla_ba_b, gla_norm_g, conv_w, od_w_in, od_w_out, q_norm_g, k_norm_g, attn_sink, s5_lam_re, s5_lam_im, s5_log_step, s5_b_re, s5_b_im, s5_c_re, s5_c_im, s5_d, s5_glu_w, s5_glu_b, router_w, router_b, exp_w_gu, exp_b_gu, exp_w_down, exp_b_down):
    p = dict(ada_w=ada_w, ada_b=ada_b, norm1_g=norm1_g, norm2_g=norm2_g,
             ev_w_in=ev_w_in, ev_w_out=ev_w_out, gla_wa_f=gla_wa_f, gla_ba_f=gla_ba_f,
             gla_wa_b=gla_wa_b, gla_ba_b=gla_ba_b, gla_norm_g=gla_norm_g, conv_w=conv_w,
             od_w_in=od_w_in, od_w_out=od_w_out, q_norm_g=q_norm_g, k_norm_g=k_norm_g,
             attn_sink=attn_sink, s5_lam_re=s5_lam_re, s5_lam_im=s5_lam_im, s5_log_step=s5_log_step,
             s5_b_re=s5_b_re, s5_b_im=s5_b_im, s5_c_re=s5_c_re, s5_c_im=s5_c_im, s5_d=s5_d,
             s5_glu_w=s5_glu_w, s5_glu_b=s5_glu_b, router_w=router_w, router_b=router_b,
             exp_w_gu=exp_w_gu, exp_b_gu=exp_b_gu, exp_w_down=exp_w_down, exp_b_down=exp_b_down)
    bp, tp, _ = x_prompt.shape
    bs, ts, _ = x_sample.shape
    seg = math.gcd(tp, ts)
    per_p, per_s = tp // seg, ts // seg
    assert bs == 1 and per_p == 1 and per_s == 2, "segment layout: prompt sequences of one segment, one sample sequence of two"
    nseg = bp * per_p + bs * per_s
    segs = Segs(seg=seg, nseg=nseg, join=bp * per_p + 1)
    x = jnp.concatenate([x_prompt.reshape(-1, D_MODEL), x_sample.reshape(-1, D_MODEL)], axis=0)
    c = jnp.concatenate([c_prompt, jnp.repeat(c_sample, per_s, axis=0)], axis=0)
    y = trunk(x, c, p, segs)
    n_p = bp * tp
    return y[:n_p].reshape(x_prompt.shape), y[n_p:].reshape(x_sample.shape)
```

```python
import functools
import math
from typing import NamedTuple

import jax
import jax.numpy as jnp
from jax import lax
from jax.experimental import pallas as pl
from jax.experimental.pallas import tpu as pltpu

F32 = jnp.float32
BF16 = jnp.bfloat16

D_MODEL = 1024
DEPTH = 4
HALF = 512
HEAD_DIM = 64

GLA_HEADS = 4
GLA_DV = 128
GLA_DK = 64
GLA_RANK = 16
GLA_TAU = 16.0
GLA_CHUNK = 64

CONV_WIDTH = 3

ATT_Q_HEADS = 8
ATT_KV_HEADS = 2
ATT_WINDOW = 128

S5_GROUP = 16
S5_GROUPS = 32
S5_STATE = 64
S5_CHUNK = 16

N_EXPERTS = 32
TOP_K = 4
SWIGLU_LIMIT = 7.0
SWIGLU_ALPHA = 1.702
MOE_BLOCK = 512

NORM_EPS = 1e-6
NEG_BIG = -1e30

LANES = 128
ROW_SLABS = D_MODEL // LANES
GATHER_UNROLL = 8

VMEM_LIMIT_BYTES = 52 * 1024 * 1024

NT_DIMS = (((1,), (1,)), ((), ()))
TN_DIMS = (((0,), (0,)), ((), ()))


class Segs(NamedTuple):
    seg: int
    nseg: int
    join: int


def _params(n_axes=1):
    return pltpu.CompilerParams(dimension_semantics=("arbitrary",) * n_axes,
                                vmem_limit_bytes=VMEM_LIMIT_BYTES)


def _split_bf16(a):
    hi = a.astype(BF16)
    lo = (a - hi.astype(F32)).astype(BF16)
    return hi, lo


def _dot(a, b):
    return jnp.dot(a, b, preferred_element_type=F32)


def _dot_split(a, b):
    a_hi, a_lo = _split_bf16(a)
    b_hi, b_lo = _split_bf16(b)
    return _dot(a_hi, b_hi) + _dot(a_lo, b_hi) + _dot(a_hi, b_lo)


def _sublane_transpose8(vs):
    axis = vs[0].ndim - 2
    sub = lax.broadcasted_iota(jnp.int32, vs[0].shape, axis)
    vs = list(vs)
    for d in (4, 2, 1):
        low = (sub & d) == 0
        out = list(vs)
        for i in range(8):
            if i & d == 0:
                out[i] = jnp.where(low, vs[i], pltpu.roll(vs[i + d], d, axis=axis))
                out[i + d] = jnp.where(low, pltpu.roll(vs[i], 8 - d, axis=axis), vs[i + d])
        vs = out
    return vs


def _slabs_to_rows(slabs):
    rows = slabs.shape[0]
    groups = slabs.reshape(rows // 8, 8, ROW_SLABS, LANES)
    cols = _sublane_transpose8([groups[:, i] for i in range(8)])
    return jnp.concatenate([c.reshape(rows, LANES) for c in cols], axis=1)


def _rows_to_slabs(value):
    rows = value.shape[0]
    cols = [value[:, s * LANES:(s + 1) * LANES].reshape(rows // 8, 8, LANES) for s in range(ROW_SLABS)]
    tiles = _sublane_transpose8(cols)
    return jnp.stack(tiles, axis=1).reshape(rows, ROW_SLABS, LANES)


def _rms_mod(x, g, sc, sh):
    ms = jnp.mean(x * x, axis=-1, keepdims=True)
    h = x * lax.rsqrt(ms + NORM_EPS) * g
    return h * (1.0 + sc) + sh


def _ada_kernel(c_ref, w_ref, b_ref, o_ref):
    c = c_ref[...]
    s = c * jax.nn.sigmoid(c)
    o_ref[0] = _dot_split(s, w_ref[0]) + b_ref[0]


def ada_modulation(c_pad, ada_w, ada_b):
    rows = c_pad.shape[0]
    cols = 6 * D_MODEL
    tn = 1536
    return pl.pallas_call(
        _ada_kernel,
        grid=(DEPTH, cols // tn),
        in_specs=[pl.BlockSpec((rows, D_MODEL), lambda l, j: (0, 0)),
                  pl.BlockSpec((1, D_MODEL, tn), lambda l, j: (l, 0, j)),
                  pl.BlockSpec((1, 1, tn), lambda l, j: (l, 0, j))],
        out_specs=pl.BlockSpec((1, rows, tn), lambda l, j: (l, 0, j)),
        out_shape=jax.ShapeDtypeStruct((DEPTH, rows, cols), F32),
        compiler_params=_params(2),
        name="ada_modulation",
    )(c_pad, ada_w, ada_b.reshape(DEPTH, 1, cols))


def _norm_proj_kernel(x_ref, g_ref, sc_ref, sh_ref, *refs, n_w):
    h = _rms_mod(x_ref[...], g_ref[...], sc_ref[0], sh_ref[0]).astype(BF16)
    for w_ref, o_ref in zip(refs[:n_w], refs[n_w:]):
        o_ref[...] = _dot(h, w_ref[...]).astype(o_ref.dtype)


def norm_proj(x, g, sc, sh, weights, segs, tm=512):
    n = x.shape[0]
    n_w = len(weights)
    seg_of = lambda i: (i * tm) // segs.seg
    in_specs = [pl.BlockSpec((tm, D_MODEL), lambda i: (i, 0)),
                pl.BlockSpec((1, D_MODEL), lambda i: (0, 0)),
                pl.BlockSpec((1, 1, D_MODEL), lambda i: (seg_of(i), 0, 0)),
                pl.BlockSpec((1, 1, D_MODEL), lambda i: (seg_of(i), 0, 0))]
    in_specs += [pl.BlockSpec(w.shape, lambda i: (0, 0)) for w in weights]
    return pl.pallas_call(
        functools.partial(_norm_proj_kernel, n_w=n_w),
        grid=(n // tm,),
        in_specs=in_specs,
        out_specs=[pl.BlockSpec((tm, w.shape[1]), lambda i: (i, 0)) for w in weights],
        out_shape=[jax.ShapeDtypeStruct((n, w.shape[1]), F32) for w in weights],
        compiler_params=_params(),
        name="norm_proj",
    )(x, g, sc, sh, *weights)


def _gla_kernel(qk_f, v_f, lr_f, qk_b, v_b, lr_b, wa_f, ba_f, wa_b, ba_b, cum_f, cum_b,
                of_ref, ob_ref, st_f, st_b, *, segs, rows):
    i = pl.program_id(0)
    nb = pl.num_programs(0)
    ib = nb - 1 - i
    bps = segs.seg // rows
    seg_f = i // bps
    seg_b = ib // bps

    @pl.when((i % bps == 0) & (seg_f != segs.join))
    def _():
        st_f[...] = jnp.zeros_like(st_f)

    @pl.when((ib % bps == bps - 1) & (seg_b + 1 != segs.join))
    def _():
        st_b[...] = jnp.zeros_like(st_b)

    L = GLA_CHUNK
    scale = GLA_DK ** -0.5
    n_chunks = rows // L
    dkw = GLA_HEADS * GLA_DK

    def direction(qk_ref, v_ref, lr_ref, wa_ref, ba_ref, cum_ref, st_ref, o_ref, reverse):
        qk = qk_ref[...]
        q = qk[:, :dkw] * scale
        k = qk[:, dkw:]
        v = v_ref[...].astype(BF16)
        pre = _dot_split(lr_ref[...], wa_ref[...]) + ba_ref[...]
        log_sig = jnp.minimum(pre, 0.0) - jnp.log(1.0 + jnp.exp(-jnp.abs(pre)))
        g_hi, g_lo = _split_bf16(log_sig * (1.0 / GLA_TAU))
        cum = cum_ref[...]
        b = _dot(cum, g_hi) + _dot(cum, g_lo)
        edge = 0 if reverse else L - 1
        b_edge = jnp.concatenate([jnp.broadcast_to(b[c * L + edge:c * L + edge + 1, :], (L, dkw))
                                  for c in range(n_chunks)], axis=0)
        qe = (q * jnp.exp(b)).astype(BF16)
        ke = (k * jnp.exp(-b)).astype(BF16)
        kd = (k * jnp.exp(b_edge - b)).astype(BF16)
        d_edge = jnp.exp(b_edge)
        mask = cum != 0
        order = range(n_chunks - 1, -1, -1) if reverse else range(n_chunks)
        outs = []
        for h in range(GLA_HEADS):
            ks = slice(h * GLA_DK, (h + 1) * GLA_DK)
            qeh, keh, kdh = qe[:, ks], ke[:, ks], kd[:, ks]
            vh = v[:, h * GLA_DV:(h + 1) * GLA_DV]
            att = lax.dot_general(qeh, keh, NT_DIMS, preferred_element_type=F32)
            o = _dot(jnp.where(mask, att, 0.0).astype(BF16), vh)
            st = st_ref[h]
            inter = [None] * n_chunks
            for c in order:
                rs = slice(c * L, (c + 1) * L)
                inter[c] = lax.dot_general(qeh[rs], st.astype(BF16), NT_DIMS, preferred_element_type=F32)
                st = st * d_edge[c * L:c * L + 1, ks] + lax.dot_general(vh[rs], kdh[rs], TN_DIMS,
                                                                        preferred_element_type=F32)
            st_ref[h] = st
            outs.append(o + jnp.concatenate(inter, axis=0))
        o_ref[...] = jnp.concatenate(outs, axis=1)

    direction(qk_f, v_f, lr_f, wa_f, ba_f, cum_f, st_f, of_ref, False)
    direction(qk_b, v_b, lr_b, wa_b, ba_b, cum_b, st_b, ob_ref, True)


def gla_bidir(a_proj, lr_proj, wa_f, ba_f, wa_b, ba_b, segs, rows=256):
    n = a_proj.shape[0]
    nb = n // rows
    lanes = lr_proj.shape[1]
    wa_f_pad = jnp.zeros((lanes, GLA_HEADS * GLA_DK), F32).at[:GLA_RANK].set(wa_f)
    wa_b_pad = jnp.zeros((lanes, GLA_HEADS * GLA_DK), F32).at[GLA_RANK:2 * GLA_RANK].set(wa_b)
    t_io = jnp.arange(rows, dtype=jnp.int32)
    same_chunk = (t_io[:, None] // GLA_CHUNK) == (t_io[None, :] // GLA_CHUNK)
    cum_f = (same_chunk & (t_io[None, :] <= t_io[:, None])).astype(BF16)
    cum_b = (same_chunk & (t_io[None, :] >= t_io[:, None])).astype(BF16)
    fwd = lambda i: (i, 0)
    bwd = lambda i: (nb - 1 - i, 0)
    const = lambda i: (0, 0)
    spec = lambda width, col, row_map: pl.BlockSpec((rows, width), lambda i: (row_map(i)[0], col))
    return pl.pallas_call(
        functools.partial(_gla_kernel, segs=segs, rows=rows),
        grid=(nb,),
        in_specs=[spec(512, 0, fwd), spec(512, 1, fwd), pl.BlockSpec((rows, lanes), fwd),
                  spec(512, 0, bwd), spec(512, 1, bwd), pl.BlockSpec((rows, lanes), bwd),
                  pl.BlockSpec(wa_f_pad.shape, const), pl.BlockSpec((1, 256), const),
                  pl.BlockSpec(wa_b_pad.shape, const), pl.BlockSpec((1, 256), const),
                  pl.BlockSpec((rows, rows), const), pl.BlockSpec((rows, rows), const)],
        out_specs=[pl.BlockSpec((rows, HALF), fwd), pl.BlockSpec((rows, HALF), bwd)],
        out_shape=[jax.ShapeDtypeStruct((n, HALF), F32)] * 2,
        scratch_shapes=[pltpu.VMEM((GLA_HEADS, GLA_DV, GLA_DK), F32)] * 2,
        compiler_params=_params(),
        name="gla_bidir",
    )(a_proj, a_proj, lr_proj, a_proj, a_proj, lr_proj,
      wa_f_pad, ba_f.reshape(1, -1), wa_b_pad, ba_b.reshape(1, -1), cum_f, cum_b)


def _even_out_kernel(x_ref, of_ref, ob_ref, og_ref, c_ref, cprev_ref, cnext_ref, ng_ref, cw_ref,
                     w_ref, g1_ref, o_ref, *, segs, tm):
    i = pl.program_id(0)
    tps = segs.seg // tm
    seg = i // tps
    has_prev = jnp.logical_not((i % tps == 0) & (seg != segs.join))
    has_next = jnp.logical_not((i % tps == tps - 1) & (seg + 1 != segs.join))

    o = of_ref[...] + ob_ref[...]
    og = og_ref[...]
    parts = []
    for h in range(GLA_HEADS):
        oh = o[:, h * GLA_DV:(h + 1) * GLA_DV]
        ms = jnp.mean(oh * oh, axis=-1, keepdims=True)
        parts.append(oh * lax.rsqrt(ms + NORM_EPS) * ng_ref[...])
    a_out = jnp.concatenate(parts, axis=1) * (og * jax.nn.sigmoid(og))

    c = c_ref[...]
    bg = c[:, :HALF]
    u = c[:, HALF:2 * HALF] * c[:, 2 * HALF:]
    cp = cprev_ref[7:8, :]
    cn = cnext_ref[0:1, :]
    u_prev_edge = jnp.where(has_prev, cp[:, HALF:2 * HALF] * cp[:, 2 * HALF:], 0.0)
    u_next_edge = jnp.where(has_next, cn[:, HALF:2 * HALF] * cn[:, 2 * HALF:], 0.0)
    row = lax.broadcasted_iota(jnp.int32, (tm, 1), 0)
    u_prev = jnp.where(row == 0, u_prev_edge, pltpu.roll(u, 1, axis=0))
    u_next = jnp.where(row == tm - 1, u_next_edge, pltpu.roll(u, tm - 1, axis=0))
    cw = cw_ref[...]
    b_out = bg * (cw[0:1, :] * u_prev + cw[1:2, :] * u + cw[2:3, :] * u_next)

    w = w_ref[...]
    mix = _dot(a_out.astype(BF16), w[:HALF, :]) + _dot(b_out.astype(BF16), w[HALF:, :])
    o_ref[...] = x_ref[...] + g1_ref[0] * mix


def even_out(x, o_f, o_b, a_proj, c_proj, norm_g, conv_w, w_out, g1, segs, tm=512):
    n = x.shape[0]
    last8 = n // 8 - 1
    seg_of = lambda i: (i * tm) // segs.seg
    return pl.pallas_call(
        functools.partial(_even_out_kernel, segs=segs, tm=tm),
        grid=(n // tm,),
        in_specs=[pl.BlockSpec((tm, D_MODEL), lambda i: (i, 0)),
                  pl.BlockSpec((tm, HALF), lambda i: (i, 0)),
                  pl.BlockSpec((tm, HALF), lambda i: (i, 0)),
                  pl.BlockSpec((tm, HALF), lambda i: (i, 2)),
                  pl.BlockSpec((tm, 3 * HALF), lambda i: (i, 0)),
                  pl.BlockSpec((8, 3 * HALF), lambda i: (jnp.maximum(i * (tm // 8) - 1, 0), 0)),
                  pl.BlockSpec((8, 3 * HALF), lambda i: (jnp.minimum((i + 1) * (tm // 8), last8), 0)),
                  pl.BlockSpec((1, GLA_DV), lambda i: (0, 0)),
                  pl.BlockSpec((CONV_WIDTH, HALF), lambda i: (0, 0)),
                  pl.BlockSpec((D_MODEL, D_MODEL), lambda i: (0, 0)),
                  pl.BlockSpec((1, 1, D_MODEL), lambda i: (seg_of(i), 0, 0))],
        out_specs=pl.BlockSpec((tm, D_MODEL), lambda i: (i, 0)),
        out_shape=jax.ShapeDtypeStruct((n, D_MODEL), F32),
        compiler_params=_params(),
        name="even_out",
    )(x, o_f, o_b, a_proj, c_proj, c_proj, c_proj, norm_g.reshape(1, -1), conv_w, w_out, g1)


def _attn_kernel(q_ref, kv_ref, kvp_ref, kvn_ref, qg_ref, kg_ref, sink_ref, o_ref, *, segs, tq):
    i = pl.program_id(0)
    W = ATT_WINDOW
    tps = segs.seg // tq
    seg = i // tps
    first = (i % tps == 0) & (seg != segs.join)
    last = (i % tps == tps - 1) & (seg + 1 != segs.join)

    kv_all = jnp.concatenate([kvp_ref[...], kv_ref[...], kvn_ref[...]], axis=0)
    kvw = ATT_KV_HEADS * HEAD_DIM
    k_heads, v_heads = [], []
    for h in range(ATT_KV_HEADS):
        kh = kv_all[:, h * HEAD_DIM:(h + 1) * HEAD_DIM]
        ms = jnp.mean(kh * kh, axis=-1, keepdims=True)
        k_heads.append((kh * lax.rsqrt(ms + NORM_EPS) * kg_ref[...]).astype(BF16))
        v_heads.append(kv_all[:, kvw + h * HEAD_DIM:kvw + (h + 1) * HEAD_DIM].astype(BF16))

    t_io = lax.broadcasted_iota(jnp.int32, (W, 3 * W), 0)
    j_io = lax.broadcasted_iota(jnp.int32, (W, 3 * W), 1)
    rel = j_io - W - t_io
    dist = jnp.abs(rel)
    in_window = dist <= W
    dist_f = dist.astype(F32)
    group = ATT_Q_HEADS // ATT_KV_HEADS
    n_blk = tq // W
    for blk in range(n_blk):
        valid = in_window
        if blk == 0:
            valid = valid & ((j_io >= W) | jnp.logical_not(first))
        if blk == n_blk - 1:
            valid = valid & ((j_io < 2 * W) | jnp.logical_not(last))
        q = q_ref[blk * W:(blk + 1) * W, :]
        outs = []
        for hq in range(ATT_Q_HEADS):
            kvh = hq // group
            qh = q[:, hq * HEAD_DIM:(hq + 1) * HEAD_DIM]
            ms = jnp.mean(qh * qh, axis=-1, keepdims=True)
            qn = (qh * lax.rsqrt(ms + NORM_EPS) * qg_ref[...] * (HEAD_DIM ** -0.5)).astype(BF16)
            kh = k_heads[kvh][blk * W:(blk + 3) * W, :]
            vh = v_heads[kvh][blk * W:(blk + 3) * W, :]
            s = lax.dot_general(qn, kh, NT_DIMS, preferred_element_type=F32)
            slope = 2.0 ** (-8.0 * (hq + 1) / ATT_Q_HEADS)
            s = jnp.where(valid, s - slope * dist_f, NEG_BIG)
            sk = sink_ref[hq]
            m = jnp.maximum(jnp.max(s, axis=-1, keepdims=True), sk)
            p = jnp.exp(s - m)
            denom = jnp.sum(p, axis=-1, keepdims=True) + jnp.exp(sk - m)
            outs.append(_dot(p.astype(BF16), vh) / denom)
        o_ref[blk * W:(blk + 1) * W, :] = jnp.concatenate(outs, axis=1)


def windowed_attention(q_proj, kv_proj, q_norm_g, k_norm_g, sink, segs, tq=512):
    n = q_proj.shape[0]
    W = ATT_WINDOW
    r = tq // W
    last = n // W - 1
    kvw = 2 * ATT_KV_HEADS * HEAD_DIM
    return pl.pallas_call(
        functools.partial(_attn_kernel, segs=segs, tq=tq),
        grid=(n // tq,),
        in_specs=[pl.BlockSpec((tq, HALF), lambda i: (i, 0)),
                  pl.BlockSpec((tq, kvw), lambda i: (i, 0)),
                  pl.BlockSpec((W, kvw), lambda i: (jnp.maximum(i * r - 1, 0), 0)),
                  pl.BlockSpec((W, kvw), lambda i: (jnp.minimum((i + 1) * r, last), 0)),
                  pl.BlockSpec((1, HEAD_DIM), lambda i: (0, 0)),
                  pl.BlockSpec((1, HEAD_DIM), lambda i: (0, 0)),
                  pl.BlockSpec(memory_space=pltpu.SMEM)],
        out_specs=pl.BlockSpec((tq, HALF), lambda i: (i, 0)),
        out_shape=jax.ShapeDtypeStruct((n, HALF), F32),
        compiler_params=_params(),
        name="windowed_attention",
    )(q_proj, kv_proj, kv_proj, kv_proj, q_norm_g.reshape(1, -1), k_norm_g.reshape(1, -1), sink)


def s5_tables(lam_re, lam_im, log_step, b_re, b_im, c_re, c_im, *, n_inner):
    L = S5_CHUNK
    dt = jnp.exp(log_step)[:, :, None]
    lr, li = lam_re, lam_im
    mag = jnp.exp(lr * dt)
    ar, ai = mag * jnp.cos(li * dt), mag * jnp.sin(li * dt)
    den = lr * lr + li * li
    zr = ((ar - 1.0) * lr + ai * li) / den
    zi = (ai * lr - (ar - 1.0) * li) / den
    bbr = zr[..., None] * b_re - zi[..., None] * b_im
    bbi = zr[..., None] * b_im + zi[..., None] * b_re
    tau = jnp.arange(L + 1, dtype=F32)[:, None, None, None]
    pmag = jnp.exp(lr[None] * dt[None] * tau)
    pang = li[None] * dt[None] * tau
    pr, pi = pmag * jnp.cos(pang), pmag * jnp.sin(pang)

    hp = lax.Precision.HIGHEST
    car = c_re[None] * pr[:, :, :, None, :] - c_im[None] * pi[:, :, :, None, :]
    cai = c_re[None] * pi[:, :, :, None, :] + c_im[None] * pr[:, :, :, None, :]
    kern = (jnp.einsum('ldgop,dgpi->ldgoi', car, bbr, precision=hp)
            - jnp.einsum('ldgop,dgpi->ldgoi', cai, bbi, precision=hp))
    s_idx = jnp.arange(L)[:, None]
    t_idx = jnp.arange(L)[None, :]
    lag_f = jnp.clip(t_idx - s_idx, 0, L)
    lag_b = jnp.clip(s_idx - t_idx, 0, L)
    kf = jnp.where((t_idx >= s_idx)[:, :, None, None, None], kern[:, 0][lag_f], 0.0)
    kb = jnp.where((s_idx >= t_idx)[:, :, None, None, None], kern[:, 1][lag_b], 0.0)
    m = (kf + kb).transpose(2, 0, 4, 1, 3).reshape(S5_GROUPS, L * S5_GROUP, L * S5_GROUP)

    abr = pr[..., None] * bbr[None] - pi[..., None] * bbi[None]
    abi = pr[..., None] * bbi[None] + pi[..., None] * bbr[None]
    e_f = L - 1 - jnp.arange(L)
    e_b = jnp.arange(L)
    to_rows = lambda a: a.transpose(1, 0, 3, 2).reshape(S5_GROUPS, L * S5_GROUP, S5_STATE)
    wp = jnp.concatenate([to_rows(abr[e_f, 0]), to_rows(abi[e_f, 0]),
                          to_rows(abr[e_b, 1]), to_rows(abi[e_b, 1])], axis=-1)

    o_f = jnp.arange(L) + 1
    o_b = L - jnp.arange(L)
    to_cols = lambda a: a.transpose(1, 3, 0, 2).reshape(S5_GROUPS, S5_STATE, L * S5_GROUP)
    wc = jnp.concatenate([to_cols(car[o_f, 0]), -to_cols(cai[o_f, 0]),
                          to_cols(car[o_b, 1]), -to_cols(cai[o_b, 1])], axis=1)

    steps = jnp.arange(n_inner + 1, dtype=F32)[:, None, None, None] * float(L)
    qmag = jnp.exp(lr[None] * dt[None] * steps)
    qang = li[None] * dt[None] * steps
    qr, qi = qmag * jnp.cos(qang), qmag * jnp.sin(qang)
    form_a = jnp.concatenate([qr, qr], -1)
    form_b = jnp.concatenate([-qi, qi], -1)
    dpow = jnp.stack([form_a[:, 0], form_b[:, 0], form_a[:, 1], form_b[:, 1]], axis=0)
    dpow = dpow.transpose(2, 0, 1, 3)
    return m.astype(BF16), wp.astype(BF16), wc, dpow


def _s5_kernel(x_ref, m_ref, wp_ref, wc_ref, dpow_ref, y_ref, pf_scr, pb_scr, sf_scr, sb_scr, *, segs, n_sub):
    P2 = 2 * S5_STATE
    n_inner = segs.seg // S5_CHUNK // n_sub
    rows_step = segs.nseg * n_sub
    x = x_ref[0]
    y_ref[0] = _dot(x, m_ref[0])
    p = _dot(x, wp_ref[0])
    pf_scr[...] = p[:, :P2]
    pb_scr[...] = p[:, P2:]

    def cmul(s, s_swapped, form_a, form_b):
        return s * form_a + s_swapped * form_b

    def swap(s):
        return pltpu.roll(s, S5_STATE, axis=1)

    def power(form, i):
        return dpow_ref[0, form, pl.ds(i, 1), :]

    d1 = [power(f, 1) for f in range(4)]
    dn = [power(f, n_inner) for f in range(4)]

    def block(i):
        return pl.ds(pl.multiple_of(i * rows_step, 8), rows_step)

    def local_step(i, carry):
        s_f, s_b = carry
        rf, rb = block(i), block(n_inner - 1 - i)
        sf_scr[rf, :] = s_f
        sb_scr[rb, :] = s_b
        return (cmul(s_f, swap(s_f), d1[0], d1[1]) + pf_scr[rf, :],
                cmul(s_b, swap(s_b), d1[2], d1[3]) + pb_scr[rb, :])
    zeros = jnp.zeros((rows_step, P2), F32)
    end_f, end_b = lax.fori_loop(0, n_inner, local_step, (zeros, zeros))

    r_io = lax.broadcasted_iota(jnp.int32, (rows_step, 1), 0)
    j_io, seg_io = r_io % n_sub, r_io // n_sub
    takes_prev = jnp.logical_not((j_io == 0) & (seg_io != segs.join))
    takes_next = jnp.logical_not((j_io == n_sub - 1) & (seg_io + 1 != segs.join))
    end_f_prev = pltpu.roll(end_f, 1, axis=0)
    end_b_next = pltpu.roll(end_b, rows_step - 1, axis=0)
    c_f, c_b = zeros, zeros
    longest = n_sub * (2 if segs.join >= 0 else 1)
    for _ in range(longest - 1):
        c_prev = pltpu.roll(c_f, 1, axis=0)
        c_f = jnp.where(takes_prev, cmul(c_prev, swap(c_prev), dn[0], dn[1]) + end_f_prev, 0.0)
        c_next = pltpu.roll(c_b, rows_step - 1, axis=0)
        c_b = jnp.where(takes_next, cmul(c_next, swap(c_next), dn[2], dn[3]) + end_b_next, 0.0)

    c_f_sw, c_b_sw = swap(c_f), swap(c_b)

    def fix_step(i, carry):
        rf, rb = block(i), block(n_inner - 1 - i)
        sf_scr[rf, :] += cmul(c_f, c_f_sw, power(0, i), power(1, i))
        sb_scr[rb, :] += cmul(c_b, c_b_sw, power(2, i), power(3, i))
        return carry
    lax.fori_loop(0, n_inner, fix_step, 0)

    wc_hi, wc_lo = _split_bf16(wc_ref[0])
    acc = y_ref[0]
    for s_scr, rows in ((sf_scr, slice(0, P2)), (sb_scr, slice(P2, 2 * P2))):
        s_hi, s_lo = _split_bf16(s_scr[...])
        acc = acc + _dot(s_hi, wc_hi[rows]) + _dot(s_lo, wc_hi[rows]) + _dot(s_hi, wc_lo[rows])
    y_ref[0] = acc


def s5_conv(du, params, segs, n_sub=8):
    n = du.shape[0]
    L = S5_CHUNK
    rows = n // L
    width = L * S5_GROUP
    n_inner = segs.seg // L // n_sub
    m, wp, wc, dpow = s5_tables(*params, n_inner=n_inner)
    xg = du.astype(BF16).reshape(segs.nseg, n_sub, n_inner, L, S5_GROUPS, S5_GROUP)
    xg = xg.transpose(4, 2, 0, 1, 3, 5).reshape(S5_GROUPS, rows, width)
    grp = lambda g: (g, 0, 0)
    yg = pl.pallas_call(
        functools.partial(_s5_kernel, segs=segs, n_sub=n_sub),
        grid=(S5_GROUPS,),
        in_specs=[pl.BlockSpec((1, rows, width), grp),
                  pl.BlockSpec((1, width, width), grp),
                  pl.BlockSpec((1, width, 4 * S5_STATE), grp),
                  pl.BlockSpec((1, 4 * S5_STATE, width), grp),
                  pl.BlockSpec((1, 4, n_inner + 1, 2 * S5_STATE), lambda g: (g, 0, 0, 0))],
        out_specs=pl.BlockSpec((1, rows, width), grp),
        out_shape=jax.ShapeDtypeStruct((S5_GROUPS, rows, width), F32),
        scratch_shapes=[pltpu.VMEM((rows, 2 * S5_STATE), F32)] * 4,
        compiler_params=_params(),
        name="s5_conv",
    )(xg, m, wp, wc, dpow)
    yg = yg.reshape(S5_GROUPS, n_inner, segs.nseg, n_sub, L, S5_GROUP)
    return yg.transpose(2, 3, 1, 4, 0, 5).reshape(n, HALF)


def _odd_out_kernel(x_ref, att_ref, du_ref, ys_ref, dsk_ref, gw_ref, gb_ref, w_ref, g1_ref, o_ref):
    y = dsk_ref[...] * du_ref[...] + ys_ref[...]
    z = 0.5 * y * (1.0 + jnp.tanh(math.sqrt(2.0 / math.pi) * (y + 0.044715 * (y * y * y))))
    gate = jax.nn.sigmoid(_dot(z.astype(BF16), gw_ref[...]) + gb_ref[...])
    d_out = z * gate
    w = w_ref[...]
    mix = _dot(att_ref[...].astype(BF16), w[:HALF, :]) + _dot(d_out.astype(BF16), w[HALF:, :])
    o_ref[...] = x_ref[...] + g1_ref[0] * mix


def odd_out(x, att, du, ys, d_skip, glu_w, glu_b, w_out, g1, segs, tm=512):
    n = x.shape[0]
    seg_of = lambda i: (i * tm) // segs.seg
    row = lambda i: (i, 0)
    const = lambda i: (0, 0)
    return pl.pallas_call(
        _odd_out_kernel,
        grid=(n // tm,),
        in_specs=[pl.BlockSpec((tm, D_MODEL), row), pl.BlockSpec((tm, HALF), row),
                  pl.BlockSpec((tm, HALF), row), pl.BlockSpec((tm, HALF), row),
                  pl.BlockSpec((1, HALF), const), pl.BlockSpec((HALF, HALF), const),
                  pl.BlockSpec((1, HALF), const), pl.BlockSpec((D_MODEL, D_MODEL), const),
                  pl.BlockSpec((1, 1, D_MODEL), lambda i: (seg_of(i), 0, 0))],
        out_specs=pl.BlockSpec((tm, D_MODEL), row),
        out_shape=jax.ShapeDtypeStruct((n, D_MODEL), F32),
        compiler_params=_params(),
        name="odd_out",
    )(x, att, du, ys, d_skip.reshape(1, -1), glu_w, glu_b.reshape(1, -1), w_out, g1)


def _norm_router_kernel(x_ref, g_ref, sc_ref, sh_ref, wr_ref, br_ref, before_ref,
                        h_ref, idx_ref, gate_ref, rank_ref, count_ref):
    @pl.when(pl.program_id(0) == 0)
    def _():
        count_ref[...] = jnp.zeros_like(count_ref)

    h = _rms_mod(x_ref[...], g_ref[...], sc_ref[0], sh_ref[0])
    h_ref[...] = _rows_to_slabs(h)
    h_hi, h_lo = _split_bf16(h)
    w_hi, w_lo = _split_bf16(wr_ref[...])
    nt = lambda a, b: lax.dot_general(a, b, NT_DIMS, preferred_element_type=F32)
    logits = nt(w_hi, h_hi) + nt(w_lo, h_hi) + nt(w_hi, h_lo) + br_ref[...]
    e_io = lax.broadcasted_iota(jnp.int32, logits.shape, 0)
    tops, picks = [], []
    for k in range(TOP_K):
        m = jnp.max(logits, axis=0, keepdims=True)
        idx = jnp.min(jnp.where(logits == m, e_io, N_EXPERTS), axis=0, keepdims=True)
        idx_ref[k:k + 1, :] = idx
        pick = e_io == idx
        logits = jnp.where(pick, -jnp.inf, logits)
        tops.append(m)
        picks.append(pick)
    es = [jnp.exp(t - tops[0]) for t in tops]
    total = es[0] + es[1] + es[2] + es[3]
    for k in range(TOP_K):
        gate_ref[k:k + 1, :] = es[k] / total

    chosen = (picks[0] | picks[1] | picks[2] | picks[3]).astype(F32)
    rank = _dot(chosen.astype(BF16), before_ref[...]) + count_ref[:, 0:1]
    for k in range(TOP_K):
        rank_ref[k:k + 1, :] = jnp.sum(jnp.where(picks[k], rank, 0.0), axis=0, keepdims=True).astype(jnp.int32)
    count_ref[...] = count_ref[...] + jnp.sum(chosen, axis=1, keepdims=True)


def norm_router(x, g, sc, sh, router_w, router_b, segs, tm=512):
    n = x.shape[0]
    seg_of = lambda i: (i * tm) // segs.seg
    t_io = jnp.arange(tm, dtype=jnp.int32)
    before = (t_io[:, None] < t_io[None, :]).astype(BF16)
    return pl.pallas_call(
        _norm_router_kernel,
        grid=(n // tm,),
        in_specs=[pl.BlockSpec((tm, D_MODEL), lambda i: (i, 0)),
                  pl.BlockSpec((1, D_MODEL), lambda i: (0, 0)),
                  pl.BlockSpec((1, 1, D_MODEL), lambda i: (seg_of(i), 0, 0)),
                  pl.BlockSpec((1, 1, D_MODEL), lambda i: (seg_of(i), 0, 0)),
                  pl.BlockSpec((N_EXPERTS, D_MODEL), lambda i: (0, 0)),
                  pl.BlockSpec((N_EXPERTS, 1), lambda i: (0, 0)),
                  pl.BlockSpec((tm, tm), lambda i: (0, 0))],
        out_specs=[pl.BlockSpec((tm, ROW_SLABS, LANES), lambda i: (i, 0, 0)),
                   pl.BlockSpec((TOP_K, tm), lambda i: (0, i)),
                   pl.BlockSpec((TOP_K, tm), lambda i: (0, i)),
                   pl.BlockSpec((TOP_K, tm), lambda i: (0, i)),
                   pl.BlockSpec((N_EXPERTS, LANES), lambda i: (0, 0))],
        out_shape=[jax.ShapeDtypeStruct((n, ROW_SLABS, LANES), F32),
                   jax.ShapeDtypeStruct((TOP_K, n), jnp.int32),
                   jax.ShapeDtypeStruct((TOP_K, n), F32),
                   jax.ShapeDtypeStruct((TOP_K, n), jnp.int32),
                   jax.ShapeDtypeStruct((N_EXPERTS, LANES), F32)],
        compiler_params=_params(),
        name="norm_router",
    )(x, g, sc, sh, router_w.T, router_b.reshape(-1, 1), before)


def moe_routing(top_idx_t, rank_t, counts, n_tiles):
    padded = (counts + MOE_BLOCK - 1) // MOE_BLOCK * MOE_BLOCK
    padded_end = jnp.cumsum(padded)
    padded_start = padded_end - padded
    experts = jnp.arange(N_EXPERTS, dtype=jnp.int32)
    start_of = jnp.sum(jnp.where(top_idx_t[:, :, None] == experts, padded_start, 0), axis=-1)
    dest = start_of + rank_t
    n_used = (padded_end[-1] // MOE_BLOCK).astype(jnp.int32)
    tile_start = jnp.arange(n_tiles, dtype=jnp.int32) * MOE_BLOCK
    tile_e = jnp.sum((padded_end[None, :] <= tile_start[:, None]).astype(jnp.int32), axis=1)
    tile_e = jnp.minimum(tile_e, N_EXPERTS - 1)
    last_e = tile_e[jnp.maximum(n_used - 1, 0)]
    tile_e = jnp.where(jnp.arange(n_tiles) < n_used, tile_e, last_e)
    n_rows = jnp.full((1,), n_tiles * MOE_BLOCK, jnp.int32)
    pad_rows = jnp.stack([jnp.concatenate([padded_start + counts, padded_end[-1:]]),
                          jnp.concatenate([padded_end, n_rows])])
    return dest, pad_rows, tile_e, n_used.reshape(1)


def _dispatch_kernel(pad_ref, dest_ref, h_ref, rows_hbm, stage, zero_row, sems, pad_sem, *, tm):
    i = pl.program_id(0)
    last = pl.num_programs(0) - 1
    slot = i % 2

    def wait_slot(s):
        for _ in range(TOP_K):
            pltpu.make_async_copy(stage.at[s], rows_hbm.at[pl.ds(0, tm)], sems.at[s]).wait()

    @pl.when(i == 0)
    def _():
        zero_row[...] = jnp.zeros_like(zero_row)

        def per_expert(e, carry):
            lo, hi = pad_ref[0, e], pad_ref[1, e]

            def start(r, c):
                pltpu.make_async_copy(zero_row, rows_hbm.at[pl.ds(r, 1)], pad_sem).start()
                return c

            def wait(r, c):
                pltpu.make_async_copy(zero_row, rows_hbm.at[pl.ds(r, 1)], pad_sem).wait()
                return c
            lax.fori_loop(lo, hi, start, 0)
            lax.fori_loop(lo, hi, wait, 0)
            return carry
        lax.fori_loop(0, N_EXPERTS + 1, per_expert, 0)

    @pl.when(i >= 2)
    def _():
        wait_slot(slot)

    stage[slot] = h_ref[...]

    def issue(group, carry):
        for u in range(GATHER_UNROLL):
            j = group * GATHER_UNROLL + u
            row = j & (tm - 1)
            copy = pltpu.make_async_copy(stage.at[slot, pl.ds(row, 1)],
                                         rows_hbm.at[pl.ds(dest_ref[0, 0, j], 1)], sems.at[slot])
            copy.start(priority=u % 2)
        return carry
    lax.fori_loop(0, TOP_K * tm // GATHER_UNROLL, issue, 0)

    @pl.when(i == last)
    def _():
        @pl.when(i >= 1)
        def _():
            wait_slot(1 - slot)
        wait_slot(slot)


def moe_dispatch(h, dest, pad_rows, n_rows, tm=512):
    n = h.shape[0]
    nt = n // tm
    dest_tiles = dest.reshape(TOP_K, nt, tm).transpose(1, 0, 2).reshape(nt, 1, TOP_K * tm)
    grid_spec = pltpu.PrefetchScalarGridSpec(
        num_scalar_prefetch=1,
        grid=(nt,),
        in_specs=[pl.BlockSpec((1, 1, TOP_K * tm), lambda i, pad: (i, 0, 0), memory_space=pltpu.SMEM),
                  pl.BlockSpec((tm, ROW_SLABS, LANES), lambda i, pad: (i, 0, 0))],
        out_specs=pl.BlockSpec(memory_space=pl.ANY),
        scratch_shapes=[pltpu.VMEM((2, tm, ROW_SLABS, LANES), F32), pltpu.VMEM((1, ROW_SLABS, LANES), F32),
                        pltpu.SemaphoreType.DMA((2,)), pltpu.SemaphoreType.DMA(())],
    )
    return pl.pallas_call(
        functools.partial(_dispatch_kernel, tm=tm),
        grid_spec=grid_spec,
        out_shape=jax.ShapeDtypeStruct((n_rows, ROW_SLABS, LANES), F32),
        compiler_params=_params(),
        name="moe_dispatch",
    )(pad_rows, dest_tiles, h)


def _start_row_gather(src_hbm, dst_vmem, sem, idx_ref, n_rows, n_priorities=1):
    def issue(group, carry):
        for u in range(GATHER_UNROLL):
            r = group * GATHER_UNROLL + u
            copy = pltpu.make_async_copy(src_hbm.at[pl.ds(idx_ref[0, 0, r], 1)], dst_vmem.at[pl.ds(r, 1)], sem)
            copy.start(priority=u % n_priorities)
        return carry
    lax.fori_loop(0, n_rows // GATHER_UNROLL, issue, 0)


def _wait_row_gather(src_hbm, dst_vmem, sem, n_rows):
    pltpu.make_async_copy(src_hbm.at[pl.ds(0, n_rows)], dst_vmem, sem).wait()


def _expert_kernel(tile_e_ref, n_used_ref, x_ref, wgu_ref, bgu_ref, wd_ref, bd_ref, y_ref):
    t = pl.program_id(0)
    n_used = n_used_ref[0]

    @pl.when(t < n_used)
    def _():
        x = _slabs_to_rows(x_ref[...]).astype(BF16)
        gu = _dot(x, wgu_ref[0]) + bgu_ref[0]
        d_ff = gu.shape[1] // 2
        g_ = jnp.minimum(gu[:, :d_ff], SWIGLU_LIMIT)
        u_ = jnp.clip(gu[:, d_ff:], -SWIGLU_LIMIT, SWIGLU_LIMIT)
        hh = g_ * jax.nn.sigmoid(SWIGLU_ALPHA * g_) * (u_ + 1.0)
        y_ref[...] = _rows_to_slabs(_dot(hh.astype(BF16), wd_ref[0]) + bd_ref[0])

    @pl.when(t >= n_used)
    def _():
        y_ref[...] = jnp.zeros_like(y_ref)


def moe_experts(x_rows, tile_e, n_used, w_gu, b_gu, w_down, b_down):
    n_tiles = tile_e.shape[0]
    d_ff2 = w_gu.shape[2]
    grid_spec = pltpu.PrefetchScalarGridSpec(
        num_scalar_prefetch=2,
        grid=(n_tiles,),
        in_specs=[pl.BlockSpec((MOE_BLOCK, ROW_SLABS, LANES), lambda t, te, nu: (t, 0, 0)),
                  pl.BlockSpec((1, D_MODEL, d_ff2), lambda t, te, nu: (te[t], 0, 0)),
                  pl.BlockSpec((1, 1, d_ff2), lambda t, te, nu: (te[t], 0, 0)),
                  pl.BlockSpec((1, d_ff2 // 2, D_MODEL), lambda t, te, nu: (te[t], 0, 0)),
                  pl.BlockSpec((1, 1, D_MODEL), lambda t, te, nu: (te[t], 0, 0))],
        out_specs=pl.BlockSpec((MOE_BLOCK, ROW_SLABS, LANES), lambda t, te, nu: (t, 0, 0)),
    )
    return pl.pallas_call(
        _expert_kernel,
        grid_spec=grid_spec,
        out_shape=jax.ShapeDtypeStruct((n_tiles * MOE_BLOCK, ROW_SLABS, LANES), F32),
        compiler_params=_params(),
        name="moe_experts",
    )(tile_e, n_used, x_rows, w_gu, b_gu, w_down, b_down)


def _combine_kernel(dest_ref, dest_next_ref, y_hbm, x_ref, gate_ref, g2_ref, o_ref, ybuf, sems, *, tm):
    i = pl.program_id(0)
    slot = i % 2
    n_rows = TOP_K * tm

    @pl.when(i == 0)
    def _():
        _start_row_gather(y_hbm, ybuf.at[0], sems.at[0], dest_ref, n_rows, n_priorities=2)

    @pl.when(i + 1 < pl.num_programs(0))
    def _():
        _start_row_gather(y_hbm, ybuf.at[1 - slot], sems.at[1 - slot], dest_next_ref, n_rows, n_priorities=2)

    _wait_row_gather(y_hbm, ybuf.at[slot], sems.at[slot], n_rows)
    gate = gate_ref[...]
    acc = gate[:, 0:1] * _slabs_to_rows(ybuf[slot, 0:tm])
    for k in range(1, TOP_K):
        acc = acc + gate[:, k:k + 1] * _slabs_to_rows(ybuf[slot, k * tm:(k + 1) * tm])
    o_ref[...] = x_ref[...] + g2_ref[0] * acc


def moe_combine(x, y_rows, dest, gate, g2, segs, tm=512):
    n = x.shape[0]
    nt = n // tm
    dest_tiles = dest.reshape(TOP_K, nt, tm).transpose(1, 0, 2).reshape(nt, 1, TOP_K * tm)
    seg_of = lambda i: (i * tm) // segs.seg
    return pl.pallas_call(
        functools.partial(_combine_kernel, tm=tm),
        grid=(nt,),
        in_specs=[pl.BlockSpec((1, 1, TOP_K * tm), lambda i: (i, 0, 0), memory_space=pltpu.SMEM),
                  pl.BlockSpec((1, 1, TOP_K * tm), lambda i: (jnp.minimum(i + 1, nt - 1), 0, 0),
                               memory_space=pltpu.SMEM),
                  pl.BlockSpec(memory_space=pl.ANY),
                  pl.BlockSpec((tm, D_MODEL), lambda i: (i, 0)),
                  pl.BlockSpec((tm, TOP_K), lambda i: (i, 0)),
                  pl.BlockSpec((1, 1, D_MODEL), lambda i: (seg_of(i), 0, 0))],
        out_specs=pl.BlockSpec((tm, D_MODEL), lambda i: (i, 0)),
        out_shape=jax.ShapeDtypeStruct((n, D_MODEL), F32),
        scratch_shapes=[pltpu.VMEM((2, TOP_K * tm, ROW_SLABS, LANES), F32), pltpu.SemaphoreType.DMA((2,))],
        compiler_params=_params(),
        name="moe_combine",
    )(dest_tiles, dest_tiles, y_rows, x, gate, g2)


def moe_layer(x, g, sc, sh, g2, router_w, router_b, w_gu, b_gu, w_down, b_down, segs):
    n = x.shape[0]
    n_tiles = -(-(n * TOP_K + N_EXPERTS * (MOE_BLOCK - 1)) // MOE_BLOCK)
    h, top_idx_t, gate_t, rank_t, counts = norm_router(x, g, sc, sh, router_w, router_b, segs)
    dest, pad_rows, tile_e, n_used = moe_routing(top_idx_t, rank_t, counts[:, 0].astype(jnp.int32), n_tiles)
    x_rows = moe_dispatch(h, dest, pad_rows, n_tiles * MOE_BLOCK)
    y_rows = moe_experts(x_rows, tile_e, n_used, w_gu, b_gu, w_down, b_down)
    return moe_combine(x, y_rows, dest, gate_t.T, g2, segs)


def _pad_cols(w, width):
    return jnp.pad(w, ((0, 0), (0, width - w.shape[1])))


def trunk(x, c, p, segs):
    nseg = segs.nseg
    rows = -(-nseg // 8) * 8
    c_pad = jnp.pad(c, ((0, rows - nseg), (0, 0)))
    mod = ada_modulation(c_pad, p['ada_w'], p['ada_b'])[:, :nseg].reshape(DEPTH, nseg, 6, 1, D_MODEL)

    for layer in range(DEPTH):
        sh1, sc1, g1, sh2, sc2, g2 = [mod[layer, :, j] for j in range(6)]
        i = layer // 2
        n1 = p['norm1_g'][layer].reshape(1, -1)
        if layer % 2 == 0:
            w_in = p['ev_w_in'][i].astype(BF16)
            w_a = w_in[:, :1536]
            w_lr = _pad_cols(w_in[:, 1536:1568], 128)
            w_c = w_in[:, 1568:]
            a_proj, lr_proj, c_proj = norm_proj(x, n1, sc1, sh1, [w_a, w_lr, w_c], segs)
            o_f, o_b = gla_bidir(a_proj, lr_proj, p['gla_wa_f'][i], p['gla_ba_f'][i],
                                 p['gla_wa_b'][i], p['gla_ba_b'][i], segs)
            x = even_out(x, o_f, o_b, a_proj, c_proj, p['gla_norm_g'][i], p['conv_w'][i],
                         p['ev_w_out'][i].astype(BF16), g1, segs)
        else:
            w_in = p['od_w_in'][i].astype(BF16)
            q_proj, kv_proj, du = norm_proj(x, n1, sc1, sh1, [w_in[:, :512], w_in[:, 512:768], w_in[:, 768:]], segs)
            att = windowed_attention(q_proj, kv_proj, p['q_norm_g'][i], p['k_norm_g'][i], p['attn_sink'][i], segs)
            s5_params = (p['s5_lam_re'][i], p['s5_lam_im'][i], p['s5_log_step'][i],
                         p['s5_b_re'][i], p['s5_b_im'][i], p['s5_c_re'][i], p['s5_c_im'][i])
            ys = s5_conv(du, s5_params, segs)
            x = odd_out(x, att, du, ys, p['s5_d'][i], p['s5_glu_w'][i].astype(BF16), p['s5_glu_b'][i],
                        p['od_w_out'][i].astype(BF16), g1, segs)
        x = moe_layer(x, p['norm2_g'][layer].reshape(1, -1), sc2, sh2, g2,
                      p['router_w'][layer], p['router_b'][layer],
                      p['exp_w_gu'][layer].astype(BF16), p['exp_b_gu'][layer].reshape(N_EXPERTS, 1, -1),
                      p['exp_w_down'][layer].astype(BF16), p['exp_b_down'][layer].reshape(N_EXPERTS, 1, -1), segs)
    return x


def kernel(x_prompt, x_sample, c_prompt, c_sample, ada_w, ada_b, norm1_g, norm2_g, ev_w_in, ev_w_out, gla_wa_f, gla_ba_f, gla_wa_b, gla_ba_b, gla_norm_g, conv_w, od_w_in, od_w_out, q_norm_g, k_norm_g, attn_sink, s5_lam_re, s5_lam_im, s5_log_step, s5_b_re, s5_b_im, s5_c_re, s5_c_im, s5_d, s5_glu_w, s5_glu_b, router_w, router_b, exp_w_gu, exp_b_gu, exp_w_down, exp_b_down):
    p = dict(ada_w=ada_w, ada_b=ada_b, norm1_g=norm1_g, norm2_g=norm2_g,
             ev_w_in=ev_w_in, ev_w_out=ev_w_out, gla_wa_f=gla_wa_f, gla_ba_f=gla_ba_f,
             gla_wa_b=gla_wa_b, gla_ba_b=gla_ba_b, gla_norm_g=gla_norm_g, conv_w=conv_w,
             od_w_in=od_w_in, od_w_out=od_w_out, q_norm_g=q_norm_g, k_norm_g=k_norm_g,
             attn_sink=attn_sink, s5_lam_re=s5_lam_re, s5_lam_im=s5_lam_im, s5_log_step=s5_log_step,
             s5_b_re=s5_b_re, s5_b_im=s5_b_im, s5_c_re=s5_c_re, s5_c_im=s5_c_im, s5_d=s5_d,
             s5_glu_w=s5_glu_w, s5_glu_b=s5_glu_b, router_w=router_w, router_b=router_b,
             exp_w_gu=exp_w_gu, exp_b_gu=exp_b_gu, exp_w_down=exp_w_down, exp_b_down=exp_b_down)
    bp, tp, _ = x_prompt.shape
    bs, ts, _ = x_sample.shape
    seg = math.gcd(tp, ts)
    per_p, per_s = tp // seg, ts // seg
    assert bs == 1 and per_p == 1 and per_s == 2, "segment layout: prompt sequences of one segment, one sample sequence of two"
    nseg = bp * per_p + bs * per_s
    segs = Segs(seg=seg, nseg=nseg, join=bp * per_p + 1)
    x = jnp.concatenate([x_prompt.reshape(-1, D_MODEL), x_sample.reshape(-1, D_MODEL)], axis=0)
    c = jnp.concatenate([c_prompt, jnp.repeat(c_sample, per_s, axis=0)], axis=0)
    y = trunk(x, c, p, segs)
    n_p = bp * tp
    return y[:n_p].reshape(x_prompt.shape), y[n_p:].reshape(x_sample.shape)
```

```python
import functools
import math
from typing import NamedTuple

import jax
import jax.numpy as jnp
from jax import lax
from jax.experimental import pallas as pl
from jax.experimental.pallas import tpu as pltpu

F32 = jnp.float32
BF16 = jnp.bfloat16

D_MODEL = 1024
DEPTH = 4
HALF = 512
HEAD_DIM = 64

GLA_HEADS = 4
GLA_DV = 128
GLA_DK = 64
GLA_RANK = 16
GLA_TAU = 16.0
GLA_CHUNK = 64

CONV_WIDTH = 3

ATT_Q_HEADS = 8
ATT_KV_HEADS = 2
ATT_WINDOW = 128

S5_GROUP = 16
S5_GROUPS = 32
S5_STATE = 64
S5_CHUNK = 16

N_EXPERTS = 32
TOP_K = 4
SWIGLU_LIMIT = 7.0
SWIGLU_ALPHA = 1.702
MOE_BLOCK = 512

NORM_EPS = 1e-6
NEG_BIG = -1e30

LANES = 128
ROW_SLABS = D_MODEL // LANES
GATHER_UNROLL = 8

VMEM_LIMIT_BYTES = 52 * 1024 * 1024

NT_DIMS = (((1,), (1,)), ((), ()))
TN_DIMS = (((0,), (0,)), ((), ()))


class Segs(NamedTuple):
    seg: int
    nseg: int
    join: int


def _params(n_axes=1):
    return pltpu.CompilerParams(dimension_semantics=("arbitrary",) * n_axes,
                                vmem_limit_bytes=VMEM_LIMIT_BYTES)


def _split_bf16(a):
    hi = a.astype(BF16)
    lo = (a - hi.astype(F32)).astype(BF16)
    return hi, lo


def _dot(a, b):
    return jnp.dot(a, b, preferred_element_type=F32)


def _dot_split(a, b):
    a_hi, a_lo = _split_bf16(a)
    b_hi, b_lo = _split_bf16(b)
    return _dot(a_hi, b_hi) + _dot(a_lo, b_hi) + _dot(a_hi, b_lo)


def _sublane_transpose8(vs):
    axis = vs[0].ndim - 2
    sub = lax.broadcasted_iota(jnp.int32, vs[0].shape, axis)
    vs = list(vs)
    for d in (4, 2, 1):
        low = (sub & d) == 0
        out = list(vs)
        for i in range(8):
            if i & d == 0:
                out[i] = jnp.where(low, vs[i], pltpu.roll(vs[i + d], d, axis=axis))
                out[i + d] = jnp.where(low, pltpu.roll(vs[i], 8 - d, axis=axis), vs[i + d])
        vs = out
    return vs


def _slabs_to_rows(slabs):
    rows = slabs.shape[0]
    groups = slabs.reshape(rows // 8, 8, ROW_SLABS, LANES)
    cols = _sublane_transpose8([groups[:, i] for i in range(8)])
    return jnp.concatenate([c.reshape(rows, LANES) for c in cols], axis=1)


def _rows_to_slabs(value):
    rows = value.shape[0]
    cols = [value[:, s * LANES:(s + 1) * LANES].reshape(rows // 8, 8, LANES) for s in range(ROW_SLABS)]
    tiles = _sublane_transpose8(cols)
    return jnp.stack(tiles, axis=1).reshape(rows, ROW_SLABS, LANES)


def _rms_mod(x, g, sc, sh):
    ms = jnp.mean(x * x, axis=-1, keepdims=True)
    h = x * lax.rsqrt(ms + NORM_EPS) * g
    return h * (1.0 + sc) + sh


def _ada_kernel(c_ref, w_ref, b_ref, o_ref):
    c = c_ref[...]
    s = c * jax.nn.sigmoid(c)
    o_ref[0] = _dot_split(s, w_ref[0]) + b_ref[0]


def ada_modulation(c_pad, ada_w, ada_b):
    rows = c_pad.shape[0]
    cols = 6 * D_MODEL
    tn = 1536
    return pl.pallas_call(
        _ada_kernel,
        grid=(DEPTH, cols // tn),
        in_specs=[pl.BlockSpec((rows, D_MODEL), lambda l, j: (0, 0)),
                  pl.BlockSpec((1, D_MODEL, tn), lambda l, j: (l, 0, j)),
                  pl.BlockSpec((1, 1, tn), lambda l, j: (l, 0, j))],
        out_specs=pl.BlockSpec((1, rows, tn), lambda l, j: (l, 0, j)),
        out_shape=jax.ShapeDtypeStruct((DEPTH, rows, cols), F32),
        compiler_params=_params(2),
        name="ada_modulation",
    )(c_pad, ada_w, ada_b.reshape(DEPTH, 1, cols))


def _norm_proj_kernel(x_ref, g_ref, sc_ref, sh_ref, *refs, n_w):
    h = _rms_mod(x_ref[...], g_ref[...], sc_ref[0], sh_ref[0]).astype(BF16)
    for w_ref, o_ref in zip(refs[:n_w], refs[n_w:]):
        o_ref[...] = _dot(h, w_ref[...]).astype(o_ref.dtype)


def norm_proj(x, g, sc, sh, weights, segs, tm=512):
    n = x.shape[0]
    n_w = len(weights)
    seg_of = lambda i: (i * tm) // segs.seg
    in_specs = [pl.BlockSpec((tm, D_MODEL), lambda i: (i, 0)),
                pl.BlockSpec((1, D_MODEL), lambda i: (0, 0)),
                pl.BlockSpec((1, 1, D_MODEL), lambda i: (seg_of(i), 0, 0)),
                pl.BlockSpec((1, 1, D_MODEL), lambda i: (seg_of(i), 0, 0))]
    in_specs += [pl.BlockSpec(w.shape, lambda i: (0, 0)) for w in weights]
    return pl.pallas_call(
        functools.partial(_norm_proj_kernel, n_w=n_w),
        grid=(n // tm,),
        in_specs=in_specs,
        out_specs=[pl.BlockSpec((tm, w.shape[1]), lambda i: (i, 0)) for w in weights],
        out_shape=[jax.ShapeDtypeStruct((n, w.shape[1]), F32) for w in weights],
        compiler_params=_params(),
        name="norm_proj",
    )(x, g, sc, sh, *weights)


def _gla_kernel(qk_f, v_f, lr_f, qk_b, v_b, lr_b, wa_f, ba_f, wa_b, ba_b, cum_f, cum_b,
                of_ref, ob_ref, st_f, st_b, *, segs, rows):
    i = pl.program_id(0)
    nb = pl.num_programs(0)
    ib = nb - 1 - i
    bps = segs.seg // rows
    seg_f = i // bps
    seg_b = ib // bps

    @pl.when((i % bps == 0) & (seg_f != segs.join))
    def _():
        st_f[...] = jnp.zeros_like(st_f)

    @pl.when((ib % bps == bps - 1) & (seg_b + 1 != segs.join))
    def _():
        st_b[...] = jnp.zeros_like(st_b)

    L = GLA_CHUNK
    scale = GLA_DK ** -0.5
    n_chunks = rows // L
    dkw = GLA_HEADS * GLA_DK

    def direction(qk_ref, v_ref, lr_ref, wa_ref, ba_ref, cum_ref, st_ref, o_ref, reverse):
        qk = qk_ref[...]
        q = qk[:, :dkw] * scale
        k = qk[:, dkw:]
        v = v_ref[...].astype(BF16)
        pre = _dot_split(lr_ref[...], wa_ref[...]) + ba_ref[...]
        log_sig = jnp.minimum(pre, 0.0) - jnp.log(1.0 + jnp.exp(-jnp.abs(pre)))
        g_hi, g_lo = _split_bf16(log_sig * (1.0 / GLA_TAU))
        cum = cum_ref[...]
        b = _dot(cum, g_hi) + _dot(cum, g_lo)
        edge = 0 if reverse else L - 1
        b_edge = jnp.concatenate([jnp.broadcast_to(b[c * L + edge:c * L + edge + 1, :], (L, dkw))
                                  for c in range(n_chunks)], axis=0)
        qe = (q * jnp.exp(b)).astype(BF16)
        ke = (k * jnp.exp(-b)).astype(BF16)
        kd = (k * jnp.exp(b_edge - b)).astype(BF16)
        d_edge = jnp.exp(b_edge)
        mask = cum != 0
        order = range(n_chunks - 1, -1, -1) if reverse else range(n_chunks)
        outs = []
        for h in range(GLA_HEADS):
            ks = slice(h * GLA_DK, (h + 1) * GLA_DK)
            qeh, keh, kdh = qe[:, ks], ke[:, ks], kd[:, ks]
            vh = v[:, h * GLA_DV:(h + 1) * GLA_DV]
            att = lax.dot_general(qeh, keh, NT_DIMS, preferred_element_type=F32)
            o = _dot(jnp.where(mask, att, 0.0).astype(BF16), vh)
            st = st_ref[h]
            inter = [None] * n_chunks
            for c in order:
                rs = slice(c * L, (c + 1) * L)
                inter[c] = lax.dot_general(qeh[rs], st.astype(BF16), NT_DIMS, preferred_element_type=F32)
                st = st * d_edge[c * L:c * L + 1, ks] + lax.dot_general(vh[rs], kdh[rs], TN_DIMS,
                                                                        preferred_element_type=F32)
            st_ref[h] = st
            outs.append(o + jnp.concatenate(inter, axis=0))
        o_ref[...] = jnp.concatenate(outs, axis=1)

    direction(qk_f, v_f, lr_f, wa_f, ba_f, cum_f, st_f, of_ref, False)
    direction(qk_b, v_b, lr_b, wa_b, ba_b, cum_b, st_b, ob_ref, True)


def gla_bidir(a_proj, lr_proj, wa_f, ba_f, wa_b, ba_b, segs, rows=256):
    n = a_proj.shape[0]
    nb = n // rows
    lanes = lr_proj.shape[1]
    wa_f_pad = jnp.zeros((lanes, GLA_HEADS * GLA_DK), F32).at[:GLA_RANK].set(wa_f)
    wa_b_pad = jnp.zeros((lanes, GLA_HEADS * GLA_DK), F32).at[GLA_RANK:2 * GLA_RANK].set(wa_b)
    t_io = jnp.arange(rows, dtype=jnp.int32)
    same_chunk = (t_io[:, None] // GLA_CHUNK) == (t_io[None, :] // GLA_CHUNK)
    cum_f = (same_chunk & (t_io[None, :] <= t_io[:, None])).astype(BF16)
    cum_b = (same_chunk & (t_io[None, :] >= t_io[:, None])).astype(BF16)
    fwd = lambda i: (i, 0)
    bwd = lambda i: (nb - 1 - i, 0)
    const = lambda i: (0, 0)
    spec = lambda width, col, row_map: pl.BlockSpec((rows, width), lambda i: (row_map(i)[0], col))
    return pl.pallas_call(
        functools.partial(_gla_kernel, segs=segs, rows=rows),
        grid=(nb,),
        in_specs=[spec(512, 0, fwd), spec(512, 1, fwd), pl.BlockSpec((rows, lanes), fwd),
                  spec(512, 0, bwd), spec(512, 1, bwd), pl.BlockSpec((rows, lanes), bwd),
                  pl.BlockSpec(wa_f_pad.shape, const), pl.BlockSpec((1, 256), const),
                  pl.BlockSpec(wa_b_pad.shape, const), pl.BlockSpec((1, 256), const),
                  pl.BlockSpec((rows, rows), const), pl.BlockSpec((rows, rows), const)],
        out_specs=[pl.BlockSpec((rows, HALF), fwd), pl.BlockSpec((rows, HALF), bwd)],
        out_shape=[jax.ShapeDtypeStruct((n, HALF), F32)] * 2,
        scratch_shapes=[pltpu.VMEM((GLA_HEADS, GLA_DV, GLA_DK), F32)] * 2,
        compiler_params=_params(),
        name="gla_bidir",
    )(a_proj, a_proj, lr_proj, a_proj, a_proj, lr_proj,
      wa_f_pad, ba_f.reshape(1, -1), wa_b_pad, ba_b.reshape(1, -1), cum_f, cum_b)


def _even_out_kernel(x_ref, of_ref, ob_ref, og_ref, c_ref, cprev_ref, cnext_ref, ng_ref, cw_ref,
                     w_ref, g1_ref, o_ref, *, segs, tm):
    i = pl.program_id(0)
    tps = segs.seg // tm
    seg = i // tps
    has_prev = jnp.logical_not((i % tps == 0) & (seg != segs.join))
    has_next = jnp.logical_not((i % tps == tps - 1) & (seg + 1 != segs.join))

    o = of_ref[...] + ob_ref[...]
    og = og_ref[...]
    parts = []
    for h in range(GLA_HEADS):
        oh = o[:, h * GLA_DV:(h + 1) * GLA_DV]
        ms = jnp.mean(oh * oh, axis=-1, keepdims=True)
        parts.append(oh * lax.rsqrt(ms + NORM_EPS) * ng_ref[...])
    a_out = jnp.concatenate(parts, axis=1) * (og * jax.nn.sigmoid(og))

    c = c_ref[...]
    bg = c[:, :HALF]
    u = c[:, HALF:2 * HALF] * c[:, 2 * HALF:]
    cp = cprev_ref[7:8, :]
    cn = cnext_ref[0:1, :]
    u_prev_edge = jnp.where(has_prev, cp[:, HALF:2 * HALF] * cp[:, 2 * HALF:], 0.0)
    u_next_edge = jnp.where(has_next, cn[:, HALF:2 * HALF] * cn[:, 2 * HALF:], 0.0)
    row = lax.broadcasted_iota(jnp.int32, (tm, 1), 0)
    u_prev = jnp.where(row == 0, u_prev_edge, pltpu.roll(u, 1, axis=0))
    u_next = jnp.where(row == tm - 1, u_next_edge, pltpu.roll(u, tm - 1, axis=0))
    cw = cw_ref[...]
    b_out = bg * (cw[0:1, :] * u_prev + cw[1:2, :] * u + cw[2:3, :] * u_next)

    w = w_ref[...]
    mix = _dot(a_out.astype(BF16), w[:HALF, :]) + _dot(b_out.astype(BF16), w[HALF:, :])
    o_ref[...] = x_ref[...] + g1_ref[0] * mix


def even_out(x, o_f, o_b, a_proj, c_proj, norm_g, conv_w, w_out, g1, segs, tm=512):
    n = x.shape[0]
    last8 = n // 8 - 1
    seg_of = lambda i: (i * tm) // segs.seg
    return pl.pallas_call(
        functools.partial(_even_out_kernel, segs=segs, tm=tm),
        grid=(n // tm,),
        in_specs=[pl.BlockSpec((tm, D_MODEL), lambda i: (i, 0)),
                  pl.BlockSpec((tm, HALF), lambda i: (i, 0)),
                  pl.BlockSpec((tm, HALF), lambda i: (i, 0)),
                  pl.BlockSpec((tm, HALF), lambda i: (i, 2)),
                  pl.BlockSpec((tm, 3 * HALF), lambda i: (i, 0)),
                  pl.BlockSpec((8, 3 * HALF), lambda i: (jnp.maximum(i * (tm // 8) - 1, 0), 0)),
                  pl.BlockSpec((8, 3 * HALF), lambda i: (jnp.minimum((i + 1) * (tm // 8), last8), 0)),
                  pl.BlockSpec((1, GLA_DV), lambda i: (0, 0)),
                  pl.BlockSpec((CONV_WIDTH, HALF), lambda i: (0, 0)),
                  pl.BlockSpec((D_MODEL, D_MODEL), lambda i: (0, 0)),
                  pl.BlockSpec((1, 1, D_MODEL), lambda i: (seg_of(i), 0, 0))],
        out_specs=pl.BlockSpec((tm, D_MODEL), lambda i: (i, 0)),
        out_shape=jax.ShapeDtypeStruct((n, D_MODEL), F32),
        compiler_params=_params(),
        name="even_out",
    )(x, o_f, o_b, a_proj, c_proj, c_proj, c_proj, norm_g.reshape(1, -1), conv_w, w_out, g1)


def _attn_kernel(q_ref, kv_ref, kvp_ref, kvn_ref, qg_ref, kg_ref, sink_ref, o_ref, *, segs, tq):
    i = pl.program_id(0)
    W = ATT_WINDOW
    tps = segs.seg // tq
    seg = i // tps
    first = (i % tps == 0) & (seg != segs.join)
    last = (i % tps == tps - 1) & (seg + 1 != segs.join)

    kv_all = jnp.concatenate([kvp_ref[...], kv_ref[...], kvn_ref[...]], axis=0)
    kvw = ATT_KV_HEADS * HEAD_DIM
    k_heads, v_heads = [], []
    for h in range(ATT_KV_HEADS):
        kh = kv_all[:, h * HEAD_DIM:(h + 1) * HEAD_DIM]
        ms = jnp.mean(kh * kh, axis=-1, keepdims=True)
        k_heads.append((kh * lax.rsqrt(ms + NORM_EPS) * kg_ref[...]).astype(BF16))
        v_heads.append(kv_all[:, kvw + h * HEAD_DIM:kvw + (h + 1) * HEAD_DIM].astype(BF16))

    t_io = lax.broadcasted_iota(jnp.int32, (W, 3 * W), 0)
    j_io = lax.broadcasted_iota(jnp.int32, (W, 3 * W), 1)
    rel = j_io - W - t_io
    dist = jnp.abs(rel)
    in_window = dist <= W
    dist_f = dist.astype(F32)
    group = ATT_Q_HEADS // ATT_KV_HEADS
    n_blk = tq // W
    for blk in range(n_blk):
        valid = in_window
        if blk == 0:
            valid = valid & ((j_io >= W) | jnp.logical_not(first))
        if blk == n_blk - 1:
            valid = valid & ((j_io < 2 * W) | jnp.logical_not(last))
        q = q_ref[blk * W:(blk + 1) * W, :]
        outs = []
        for hq in range(ATT_Q_HEADS):
            kvh = hq // group
            qh = q[:, hq * HEAD_DIM:(hq + 1) * HEAD_DIM]
            ms = jnp.mean(qh * qh, axis=-1, keepdims=True)
            qn = (qh * lax.rsqrt(ms + NORM_EPS) * qg_ref[...] * (HEAD_DIM ** -0.5)).astype(BF16)
            kh = k_heads[kvh][blk * W:(blk + 3) * W, :]
            vh = v_heads[kvh][blk * W:(blk + 3) * W, :]
            s = lax.dot_general(qn, kh, NT_DIMS, preferred_element_type=F32)
            slope = 2.0 ** (-8.0 * (hq + 1) / ATT_Q_HEADS)
            s = jnp.where(valid, s - slope * dist_f, NEG_BIG)
            sk = sink_ref[hq]
            m = jnp.maximum(jnp.max(s, axis=-1, keepdims=True), sk)
            p = jnp.exp(s - m)
            denom = jnp.sum(p, axis=-1, keepdims=True) + jnp.exp(sk - m)
            outs.append(_dot(p.astype(BF16), vh) / denom)
        o_ref[blk * W:(blk + 1) * W, :] = jnp.concatenate(outs, axis=1)


def windowed_attention(q_proj, kv_proj, q_norm_g, k_norm_g, sink, segs, tq=512):
    n = q_proj.shape[0]
    W = ATT_WINDOW
    r = tq // W
    last = n // W - 1
    kvw = 2 * ATT_KV_HEADS * HEAD_DIM
    return pl.pallas_call(
        functools.partial(_attn_kernel, segs=segs, tq=tq),
        grid=(n // tq,),
        in_specs=[pl.BlockSpec((tq, HALF), lambda i: (i, 0)),
                  pl.BlockSpec((tq, kvw), lambda i: (i, 0)),
                  pl.BlockSpec((W, kvw), lambda i: (jnp.maximum(i * r - 1, 0), 0)),
                  pl.BlockSpec((W, kvw), lambda i: (jnp.minimum((i + 1) * r, last), 0)),
                  pl.BlockSpec((1, HEAD_DIM), lambda i: (0, 0)),
                  pl.BlockSpec((1, HEAD_DIM), lambda i: (0, 0)),
                  pl.BlockSpec(memory_space=pltpu.SMEM)],
        out_specs=pl.BlockSpec((tq, HALF), lambda i: (i, 0)),
        out_shape=jax.ShapeDtypeStruct((n, HALF), F32),
        compiler_params=_params(),
        name="windowed_attention",
    )(q_proj, kv_proj, kv_proj, kv_proj, q_norm_g.reshape(1, -1), k_norm_g.reshape(1, -1), sink)


def s5_tables(lam_re, lam_im, log_step, b_re, b_im, c_re, c_im, *, n_inner):
    L = S5_CHUNK
    dt = jnp.exp(log_step)[:, :, None]
    lr, li = lam_re, lam_im
    mag = jnp.exp(lr * dt)
    ar, ai = mag * jnp.cos(li * dt), mag * jnp.sin(li * dt)
    den = lr * lr + li * li
    zr = ((ar - 1.0) * lr + ai * li) / den
    zi = (ai * lr - (ar - 1.0) * li) / den
    bbr = zr[..., None] * b_re - zi[..., None] * b_im
    bbi = zr[..., None] * b_im + zi[..., None] * b_re
    tau = jnp.arange(L + 1, dtype=F32)[:, None, None, None]
    pmag = jnp.exp(lr[None] * dt[None] * tau)
    pang = li[None] * dt[None] * tau
    pr, pi = pmag * jnp.cos(pang), pmag * jnp.sin(pang)

    hp = lax.Precision.HIGHEST
    car = c_re[None] * pr[:, :, :, None, :] - c_im[None] * pi[:, :, :, None, :]
    cai = c_re[None] * pi[:, :, :, None, :] + c_im[None] * pr[:, :, :, None, :]
    kern = (jnp.einsum('ldgop,dgpi->ldgoi', car, bbr, precision=hp)
            - jnp.einsum('ldgop,dgpi->ldgoi', cai, bbi, precision=hp))
    s_idx = jnp.arange(L)[:, None]
    t_idx = jnp.arange(L)[None, :]
    lag_f = jnp.clip(t_idx - s_idx, 0, L)
    lag_b = jnp.clip(s_idx - t_idx, 0, L)
    kf = jnp.where((t_idx >= s_idx)[:, :, None, None, None], kern[:, 0][lag_f], 0.0)
    kb = jnp.where((s_idx >= t_idx)[:, :, None, None, None], kern[:, 1][lag_b], 0.0)
    m = (kf + kb).transpose(2, 0, 4, 1, 3).reshape(S5_GROUPS, L * S5_GROUP, L * S5_GROUP)

    abr = pr[..., None] * bbr[None] - pi[..., None] * bbi[None]
    abi = pr[..., None] * bbi[None] + pi[..., None] * bbr[None]
    e_f = L - 1 - jnp.arange(L)
    e_b = jnp.arange(L)
    to_rows = lambda a: a.transpose(1, 0, 3, 2).reshape(S5_GROUPS, L * S5_GROUP, S5_STATE)
    wp = jnp.concatenate([to_rows(abr[e_f, 0]), to_rows(abi[e_f, 0]),
                          to_rows(abr[e_b, 1]), to_rows(abi[e_b, 1])], axis=-1)

    o_f = jnp.arange(L) + 1
    o_b = L - jnp.arange(L)
    to_cols = lambda a: a.transpose(1, 3, 0, 2).reshape(S5_GROUPS, S5_STATE, L * S5_GROUP)
    wc = jnp.concatenate([to_cols(car[o_f, 0]), -to_cols(cai[o_f, 0]),
                          to_cols(car[o_b, 1]), -to_cols(cai[o_b, 1])], axis=1)

    steps = jnp.arange(n_inner + 1, dtype=F32)[:, None, None, None] * float(L)
    qmag = jnp.exp(lr[None] * dt[None] * steps)
    qang = li[None] * dt[None] * steps
    qr, qi = qmag * jnp.cos(qang), qmag * jnp.sin(qang)
    form_a = jnp.concatenate([qr, qr], -1)
    form_b = jnp.concatenate([-qi, qi], -1)
    dpow = jnp.stack([form_a[:, 0], form_b[:, 0], form_a[:, 1], form_b[:, 1]], axis=0)
    dpow = dpow.transpose(2, 0, 1, 3)
    return m.astype(BF16), wp.astype(BF16), wc, dpow


def _s5_kernel(x_ref, m_ref, wp_ref, wc_ref, dpow_ref, y_ref, pf_scr, pb_scr, sf_scr, sb_scr, *, segs, n_sub):
    P2 = 2 * S5_STATE
    n_inner = segs.seg // S5_CHUNK // n_sub
    rows_step = segs.nseg * n_sub
    x = x_ref[0]
    y_local = _dot(x, m_ref[0])
    p = _dot(x, wp_ref[0])
    pf_scr[...] = p[:, :P2]
    pb_scr[...] = p[:, P2:]

    def cmul(s, s_swapped, form_a, form_b):
        return s * form_a + s_swapped * form_b

    def swap(s):
        return pltpu.roll(s, S5_STATE, axis=1)

    def power(form, i):
        return dpow_ref[0, form, pl.ds(i, 1), :]

    d1 = [power(f, 1) for f in range(4)]
    dn = [power(f, n_inner) for f in range(4)]

    def block(i):
        return pl.ds(pl.multiple_of(i * rows_step, 8), rows_step)

    def local_step(i, carry):
        s_f, s_b = carry
        rf, rb = block(i), block(n_inner - 1 - i)
        sf_scr[rf, :] = s_f
        sb_scr[rb, :] = s_b
        return (cmul(s_f, swap(s_f), d1[0], d1[1]) + pf_scr[rf, :],
                cmul(s_b, swap(s_b), d1[2], d1[3]) + pb_scr[rb, :])
    zeros = jnp.zeros((rows_step, P2), F32)
    end_f, end_b = lax.fori_loop(0, n_inner, local_step, (zeros, zeros))

    r_io = lax.broadcasted_iota(jnp.int32, (rows_step, 1), 0)
    j_io, seg_io = r_io % n_sub, r_io // n_sub
    takes_prev = jnp.logical_not((j_io == 0) & (seg_io != segs.join))
    takes_next = jnp.logical_not((j_io == n_sub - 1) & (seg_io + 1 != segs.join))
    end_f_prev = pltpu.roll(end_f, 1, axis=0)
    end_b_next = pltpu.roll(end_b, rows_step - 1, axis=0)
    c_f, c_b = zeros, zeros
    longest = n_sub * (2 if segs.join >= 0 else 1)
    for _ in range(longest - 1):
        c_prev = pltpu.roll(c_f, 1, axis=0)
        c_f = jnp.where(takes_prev, cmul(c_prev, swap(c_prev), dn[0], dn[1]) + end_f_prev, 0.0)
        c_next = pltpu.roll(c_b, rows_step - 1, axis=0)
        c_b = jnp.where(takes_next, cmul(c_next, swap(c_next), dn[2], dn[3]) + end_b_next, 0.0)

    c_f_sw, c_b_sw = swap(c_f), swap(c_b)

    def fix_step(i, carry):
        rf, rb = block(i), block(n_inner - 1 - i)
        sf_scr[rf, :] += cmul(c_f, c_f_sw, power(0, i), power(1, i))
        sb_scr[rb, :] += cmul(c_b, c_b_sw, power(2, i), power(3, i))
        return carry
    lax.fori_loop(0, n_inner, fix_step, 0)

    wc_hi, wc_lo = _split_bf16(wc_ref[0])
    acc = y_local
    for s_scr, rows in ((sf_scr, slice(0, P2)), (sb_scr, slice(P2, 2 * P2))):
        s_hi, s_lo = _split_bf16(s_scr[...])
        acc = acc + _dot(s_hi, wc_hi[rows]) + _dot(s_lo, wc_hi[rows]) + _dot(s_hi, wc_lo[rows])
    y_ref[0] = acc.astype(y_ref.dtype)


def s5_conv(du, params, segs, n_sub=8):
    n = du.shape[0]
    L = S5_CHUNK
    rows = n // L
    width = L * S5_GROUP
    n_inner = segs.seg // L // n_sub
    m, wp, wc, dpow = s5_tables(*params, n_inner=n_inner)
    xg = du.astype(BF16).reshape(segs.nseg, n_sub, n_inner, L, S5_GROUPS, S5_GROUP)
    xg = xg.transpose(4, 2, 0, 1, 3, 5).reshape(S5_GROUPS, rows, width)
    grp = lambda g: (g, 0, 0)
    yg = pl.pallas_call(
        functools.partial(_s5_kernel, segs=segs, n_sub=n_sub),
        grid=(S5_GROUPS,),
        in_specs=[pl.BlockSpec((1, rows, width), grp),
                  pl.BlockSpec((1, width, width), grp),
                  pl.BlockSpec((1, width, 4 * S5_STATE), grp),
                  pl.BlockSpec((1, 4 * S5_STATE, width), grp),
                  pl.BlockSpec((1, 4, n_inner + 1, 2 * S5_STATE), lambda g: (g, 0, 0, 0))],
        out_specs=pl.BlockSpec((1, rows, width), grp),
        out_shape=jax.ShapeDtypeStruct((S5_GROUPS, rows, width), BF16),
        scratch_shapes=[pltpu.VMEM((rows, 2 * S5_STATE), F32)] * 4,
        compiler_params=_params(),
        name="s5_conv",
    )(xg, m, wp, wc, dpow)
    yg = yg.reshape(S5_GROUPS, n_inner, segs.nseg, n_sub, L, S5_GROUP)
    return yg.transpose(2, 3, 1, 4, 0, 5).reshape(n, HALF)


def _odd_out_kernel(x_ref, att_ref, du_ref, ys_ref, dsk_ref, gw_ref, gb_ref, w_ref, g1_ref, o_ref):
    y = dsk_ref[...] * du_ref[...] + ys_ref[...].astype(F32)
    z = 0.5 * y * (1.0 + jnp.tanh(math.sqrt(2.0 / math.pi) * (y + 0.044715 * (y * y * y))))
    gate = jax.nn.sigmoid(_dot(z.astype(BF16), gw_ref[...]) + gb_ref[...])
    d_out = z * gate
    w = w_ref[...]
    mix = _dot(att_ref[...].astype(BF16), w[:HALF, :]) + _dot(d_out.astype(BF16), w[HALF:, :])
    o_ref[...] = x_ref[...] + g1_ref[0] * mix


def odd_out(x, att, du, ys, d_skip, glu_w, glu_b, w_out, g1, segs, tm=512):
    n = x.shape[0]
    seg_of = lambda i: (i * tm) // segs.seg
    row = lambda i: (i, 0)
    const = lambda i: (0, 0)
    return pl.pallas_call(
        _odd_out_kernel,
        grid=(n // tm,),
        in_specs=[pl.BlockSpec((tm, D_MODEL), row), pl.BlockSpec((tm, HALF), row),
                  pl.BlockSpec((tm, HALF), row), pl.BlockSpec((tm, HALF), row),
                  pl.BlockSpec((1, HALF), const), pl.BlockSpec((HALF, HALF), const),
                  pl.BlockSpec((1, HALF), const), pl.BlockSpec((D_MODEL, D_MODEL), const),
                  pl.BlockSpec((1, 1, D_MODEL), lambda i: (seg_of(i), 0, 0))],
        out_specs=pl.BlockSpec((tm, D_MODEL), row),
        out_shape=jax.ShapeDtypeStruct((n, D_MODEL), F32),
        compiler_params=_params(),
        name="odd_out",
    )(x, att, du, ys, d_skip.reshape(1, -1), glu_w, glu_b.reshape(1, -1), w_out, g1)


def _norm_router_kernel(x_ref, g_ref, sc_ref, sh_ref, wr_ref, br_ref, before_ref,
                        h_ref, idx_ref, gate_ref, rank_ref, count_ref):
    @pl.when(pl.program_id(0) == 0)
    def _():
        count_ref[...] = jnp.zeros_like(count_ref)

    h = _rms_mod(x_ref[...], g_ref[...], sc_ref[0], sh_ref[0])
    h_ref[...] = _rows_to_slabs(h)
    h_hi, h_lo = _split_bf16(h)
    w_hi, w_lo = _split_bf16(wr_ref[...])
    nt = lambda a, b: lax.dot_general(a, b, NT_DIMS, preferred_element_type=F32)
    logits = nt(w_hi, h_hi) + nt(w_lo, h_hi) + nt(w_hi, h_lo) + br_ref[...]
    e_io = lax.broadcasted_iota(jnp.int32, logits.shape, 0)
    tops, picks = [], []
    for k in range(TOP_K):
        m = jnp.max(logits, axis=0, keepdims=True)
        idx = jnp.min(jnp.where(logits == m, e_io, N_EXPERTS), axis=0, keepdims=True)
        idx_ref[k:k + 1, :] = idx
        pick = e_io == idx
        logits = jnp.where(pick, -jnp.inf, logits)
        tops.append(m)
        picks.append(pick)
    es = [jnp.exp(t - tops[0]) for t in tops]
    total = es[0] + es[1] + es[2] + es[3]
    for k in range(TOP_K):
        gate_ref[k:k + 1, :] = es[k] / total

    chosen = (picks[0] | picks[1] | picks[2] | picks[3]).astype(F32)
    rank = _dot(chosen.astype(BF16), before_ref[...]) + count_ref[:, 0:1]
    for k in range(TOP_K):
        rank_ref[k:k + 1, :] = jnp.sum(jnp.where(picks[k], rank, 0.0), axis=0, keepdims=True).astype(jnp.int32)
    count_ref[...] = count_ref[...] + jnp.sum(chosen, axis=1, keepdims=True)


def norm_router(x, g, sc, sh, router_w, router_b, segs, tm=512):
    n = x.shape[0]
    seg_of = lambda i: (i * tm) // segs.seg
    t_io = jnp.arange(tm, dtype=jnp.int32)
    before = (t_io[:, None] < t_io[None, :]).astype(BF16)
    return pl.pallas_call(
        _norm_router_kernel,
        grid=(n // tm,),
        in_specs=[pl.BlockSpec((tm, D_MODEL), lambda i: (i, 0)),
                  pl.BlockSpec((1, D_MODEL), lambda i: (0, 0)),
                  pl.BlockSpec((1, 1, D_MODEL), lambda i: (seg_of(i), 0, 0)),
                  pl.BlockSpec((1, 1, D_MODEL), lambda i: (seg_of(i), 0, 0)),
                  pl.BlockSpec((N_EXPERTS, D_MODEL), lambda i: (0, 0)),
                  pl.BlockSpec((N_EXPERTS, 1), lambda i: (0, 0)),
                  pl.BlockSpec((tm, tm), lambda i: (0, 0))],
        out_specs=[pl.BlockSpec((tm, ROW_SLABS, LANES), lambda i: (i, 0, 0)),
                   pl.BlockSpec((TOP_K, tm), lambda i: (0, i)),
                   pl.BlockSpec((TOP_K, tm), lambda i: (0, i)),
                   pl.BlockSpec((TOP_K, tm), lambda i: (0, i)),
                   pl.BlockSpec((N_EXPERTS, LANES), lambda i: (0, 0))],
        out_shape=[jax.ShapeDtypeStruct((n, ROW_SLABS, LANES), F32),
                   jax.ShapeDtypeStruct((TOP_K, n), jnp.int32),
                   jax.ShapeDtypeStruct((TOP_K, n), F32),
                   jax.ShapeDtypeStruct((TOP_K, n), jnp.int32),
                   jax.ShapeDtypeStruct((N_EXPERTS, LANES), F32)],
        compiler_params=_params(),
        name="norm_router",
    )(x, g, sc, sh, router_w.T, router_b.reshape(-1, 1), before)


def moe_routing(top_idx_t, rank_t, counts, n_tiles):
    padded = (counts + MOE_BLOCK - 1) // MOE_BLOCK * MOE_BLOCK
    padded_end = jnp.cumsum(padded)
    padded_start = padded_end - padded
    experts = jnp.arange(N_EXPERTS, dtype=jnp.int32)
    start_of = jnp.sum(jnp.where(top_idx_t[:, :, None] == experts, padded_start, 0), axis=-1)
    dest = start_of + rank_t
    n_used = (padded_end[-1] // MOE_BLOCK).astype(jnp.int32)
    tile_start = jnp.arange(n_tiles, dtype=jnp.int32) * MOE_BLOCK
    tile_e = jnp.sum((padded_end[None, :] <= tile_start[:, None]).astype(jnp.int32), axis=1)
    tile_e = jnp.minimum(tile_e, N_EXPERTS - 1)
    last_e = tile_e[jnp.maximum(n_used - 1, 0)]
    tile_e = jnp.where(jnp.arange(n_tiles) < n_used, tile_e, last_e)
    n_rows = jnp.full((1,), n_tiles * MOE_BLOCK, jnp.int32)
    pad_rows = jnp.stack([jnp.concatenate([padded_start + counts, padded_end[-1:]]),
                          jnp.concatenate([padded_end, n_rows])])
    return dest, pad_rows, tile_e, n_used.reshape(1)


def _dispatch_kernel(pad_ref, dest_ref, h_ref, rows_hbm, stage, zero_row, sems, pad_sem, *, tm):
    i = pl.program_id(0)
    last = pl.num_programs(0) - 1
    slot = i % 2

    def wait_slot(s):
        for _ in range(TOP_K):
            pltpu.make_async_copy(stage.at[s], rows_hbm.at[pl.ds(0, tm)], sems.at[s]).wait()

    @pl.when(i == 0)
    def _():
        zero_row[...] = jnp.zeros_like(zero_row)

        def per_expert(e, carry):
            lo, hi = pad_ref[0, e], pad_ref[1, e]

            def start(r, c):
                pltpu.make_async_copy(zero_row, rows_hbm.at[pl.ds(r, 1)], pad_sem).start()
                return c

            def wait(r, c):
                pltpu.make_async_copy(zero_row, rows_hbm.at[pl.ds(r, 1)], pad_sem).wait()
                return c
            lax.fori_loop(lo, hi, start, 0)
            lax.fori_loop(lo, hi, wait, 0)
            return carry
        lax.fori_loop(0, N_EXPERTS + 1, per_expert, 0)

    @pl.when(i >= 2)
    def _():
        wait_slot(slot)

    stage[slot] = h_ref[...]

    def issue(group, carry):
        for u in range(GATHER_UNROLL):
            j = group * GATHER_UNROLL + u
            row = j & (tm - 1)
            copy = pltpu.make_async_copy(stage.at[slot, pl.ds(row, 1)],
                                         rows_hbm.at[pl.ds(dest_ref[0, 0, j], 1)], sems.at[slot])
            copy.start(priority=u % 2)
        return carry
    lax.fori_loop(0, TOP_K * tm // GATHER_UNROLL, issue, 0)

    @pl.when(i == last)
    def _():
        @pl.when(i >= 1)
        def _():
            wait_slot(1 - slot)
        wait_slot(slot)


def moe_dispatch(h, dest, pad_rows, n_rows, tm=512):
    n = h.shape[0]
    nt = n // tm
    dest_tiles = dest.reshape(TOP_K, nt, tm).transpose(1, 0, 2).reshape(nt, 1, TOP_K * tm)
    grid_spec = pltpu.PrefetchScalarGridSpec(
        num_scalar_prefetch=1,
        grid=(nt,),
        in_specs=[pl.BlockSpec((1, 1, TOP_K * tm), lambda i, pad: (i, 0, 0), memory_space=pltpu.SMEM),
                  pl.BlockSpec((tm, ROW_SLABS, LANES), lambda i, pad: (i, 0, 0))],
        out_specs=pl.BlockSpec(memory_space=pl.ANY),
        scratch_shapes=[pltpu.VMEM((2, tm, ROW_SLABS, LANES), F32), pltpu.VMEM((1, ROW_SLABS, LANES), F32),
                        pltpu.SemaphoreType.DMA((2,)), pltpu.SemaphoreType.DMA(())],
    )
    return pl.pallas_call(
        functools.partial(_dispatch_kernel, tm=tm),
        grid_spec=grid_spec,
        out_shape=jax.ShapeDtypeStruct((n_rows, ROW_SLABS, LANES), F32),
        compiler_params=_params(),
        name="moe_dispatch",
    )(pad_rows, dest_tiles, h)


def _start_row_gather(src_hbm, dst_vmem, sem, idx_ref, n_rows, n_priorities=1):
    def issue(group, carry):
        for u in range(GATHER_UNROLL):
            r = group * GATHER_UNROLL + u
            copy = pltpu.make_async_copy(src_hbm.at[pl.ds(idx_ref[0, 0, r], 1)], dst_vmem.at[pl.ds(r, 1)], sem)
            copy.start(priority=u % n_priorities)
        return carry
    lax.fori_loop(0, n_rows // GATHER_UNROLL, issue, 0)


def _wait_row_gather(src_hbm, dst_vmem, sem, n_rows):
    pltpu.make_async_copy(src_hbm.at[pl.ds(0, n_rows)], dst_vmem, sem).wait()


def _expert_kernel(tile_e_ref, n_used_ref, x_ref, wgu_ref, bgu_ref, wd_ref, bd_ref, y_ref):
    t = pl.program_id(0)
    n_used = n_used_ref[0]

    @pl.when(t < n_used)
    def _():
        x = _slabs_to_rows(x_ref[...]).astype(BF16)
        gu = _dot(x, wgu_ref[0]) + bgu_ref[0]
        d_ff = gu.shape[1] // 2
        g_ = jnp.minimum(gu[:, :d_ff], SWIGLU_LIMIT)
        u_ = jnp.clip(gu[:, d_ff:], -SWIGLU_LIMIT, SWIGLU_LIMIT)
        hh = g_ * jax.nn.sigmoid(SWIGLU_ALPHA * g_) * (u_ + 1.0)
        y_ref[...] = _rows_to_slabs(_dot(hh.astype(BF16), wd_ref[0]) + bd_ref[0])

    @pl.when(t >= n_used)
    def _():
        y_ref[...] = jnp.zeros_like(y_ref)


def moe_experts(x_rows, tile_e, n_used, w_gu, b_gu, w_down, b_down):
    n_tiles = tile_e.shape[0]
    d_ff2 = w_gu.shape[2]
    grid_spec = pltpu.PrefetchScalarGridSpec(
        num_scalar_prefetch=2,
        grid=(n_tiles,),
        in_specs=[pl.BlockSpec((MOE_BLOCK, ROW_SLABS, LANES), lambda t, te, nu: (t, 0, 0)),
                  pl.BlockSpec((1, D_MODEL, d_ff2), lambda t, te, nu: (te[t], 0, 0)),
                  pl.BlockSpec((1, 1, d_ff2), lambda t, te, nu: (te[t], 0, 0)),
                  pl.BlockSpec((1, d_ff2 // 2, D_MODEL), lambda t, te, nu: (te[t], 0, 0)),
                  pl.BlockSpec((1, 1, D_MODEL), lambda t, te, nu: (te[t], 0, 0))],
        out_specs=pl.BlockSpec((MOE_BLOCK, ROW_SLABS, LANES), lambda t, te, nu: (t, 0, 0)),
    )
    return pl.pallas_call(
        _expert_kernel,
        grid_spec=grid_spec,
        out_shape=jax.ShapeDtypeStruct((n_tiles * MOE_BLOCK, ROW_SLABS, LANES), F32),
        compiler_params=_params(),
        name="moe_experts",
    )(tile_e, n_used, x_rows, w_gu, b_gu, w_down, b_down)


def _combine_kernel(dest_ref, dest_next_ref, y_hbm, x_ref, gate_ref, g2_ref, o_ref, ybuf, sems, *, tm):
    i = pl.program_id(0)
    slot = i % 2
    n_rows = TOP_K * tm

    @pl.when(i == 0)
    def _():
        _start_row_gather(y_hbm, ybuf.at[0], sems.at[0], dest_ref, n_rows, n_priorities=2)

    @pl.when(i + 1 < pl.num_programs(0))
    def _():
        _start_row_gather(y_hbm, ybuf.at[1 - slot], sems.at[1 - slot], dest_next_ref, n_rows, n_priorities=2)

    _wait_row_gather(y_hbm, ybuf.at[slot], sems.at[slot], n_rows)
    gate = gate_ref[...]
    acc = gate[:, 0:1] * _slabs_to_rows(ybuf[slot, 0:tm])
    for k in range(1, TOP_K):
        acc = acc + gate[:, k:k + 1] * _slabs_to_rows(ybuf[slot, k * tm:(k + 1) * tm])
    o_ref[...] = x_ref[...] + g2_ref[0] * acc


def moe_combine(x, y_rows, dest, gate, g2, segs, tm=512):
    n = x.shape[0]
    nt = n // tm
    dest_tiles = dest.reshape(TOP_K, nt, tm).transpose(1, 0, 2).reshape(nt, 1, TOP_K * tm)
    seg_of = lambda i: (i * tm) // segs.seg
    return pl.pallas_call(
        functools.partial(_combine_kernel, tm=tm),
        grid=(nt,),
        in_specs=[pl.BlockSpec((1, 1, TOP_K * tm), lambda i: (i, 0, 0), memory_space=pltpu.SMEM),
                  pl.BlockSpec((1, 1, TOP_K * tm), lambda i: (jnp.minimum(i + 1, nt - 1), 0, 0),
                               memory_space=pltpu.SMEM),
                  pl.BlockSpec(memory_space=pl.ANY),
                  pl.BlockSpec((tm, D_MODEL), lambda i: (i, 0)),
                  pl.BlockSpec((tm, TOP_K), lambda i: (i, 0)),
                  pl.BlockSpec((1, 1, D_MODEL), lambda i: (seg_of(i), 0, 0))],
        out_specs=pl.BlockSpec((tm, D_MODEL), lambda i: (i, 0)),
        out_shape=jax.ShapeDtypeStruct((n, D_MODEL), F32),
        scratch_shapes=[pltpu.VMEM((2, TOP_K * tm, ROW_SLABS, LANES), F32), pltpu.SemaphoreType.DMA((2,))],
        compiler_params=_params(),
        name="moe_combine",
    )(dest_tiles, dest_tiles, y_rows, x, gate, g2)


def moe_layer(x, g, sc, sh, g2, router_w, router_b, w_gu, b_gu, w_down, b_down, segs):
    n = x.shape[0]
    n_tiles = -(-(n * TOP_K + N_EXPERTS * (MOE_BLOCK - 1)) // MOE_BLOCK)
    h, top_idx_t, gate_t, rank_t, counts = norm_router(x, g, sc, sh, router_w, router_b, segs)
    dest, pad_rows, tile_e, n_used = moe_routing(top_idx_t, rank_t, counts[:, 0].astype(jnp.int32), n_tiles)
    x_rows = moe_dispatch(h, dest, pad_rows, n_tiles * MOE_BLOCK)
    y_rows = moe_experts(x_rows, tile_e, n_used, w_gu, b_gu, w_down, b_down)
    return moe_combine(x, y_rows, dest, gate_t.T, g2, segs)


def _pad_cols(w, width):
    return jnp.pad(w, ((0, 0), (0, width - w.shape[1])))


def trunk(x, c, p, segs):
    nseg = segs.nseg
    rows = -(-nseg // 8) * 8
    c_pad = jnp.pad(c, ((0, rows - nseg), (0, 0)))
    mod = ada_modulation(c_pad, p['ada_w'], p['ada_b'])[:, :nseg].reshape(DEPTH, nseg, 6, 1, D_MODEL)

    for layer in range(DEPTH):
        sh1, sc1, g1, sh2, sc2, g2 = [mod[layer, :, j] for j in range(6)]
        i = layer // 2
        n1 = p['norm1_g'][layer].reshape(1, -1)
        if layer % 2 == 0:
            w_in = p['ev_w_in'][i].astype(BF16)
            w_a = w_in[:, :1536]
            w_lr = _pad_cols(w_in[:, 1536:1568], 128)
            w_c = w_in[:, 1568:]
            a_proj, lr_proj, c_proj = norm_proj(x, n1, sc1, sh1, [w_a, w_lr, w_c], segs)
            o_f, o_b = gla_bidir(a_proj, lr_proj, p['gla_wa_f'][i], p['gla_ba_f'][i],
                                 p['gla_wa_b'][i], p['gla_ba_b'][i], segs)
            x = even_out(x, o_f, o_b, a_proj, c_proj, p['gla_norm_g'][i], p['conv_w'][i],
                         p['ev_w_out'][i].astype(BF16), g1, segs)
        else:
            w_in = p['od_w_in'][i].astype(BF16)
            q_proj, kv_proj, du = norm_proj(x, n1, sc1, sh1, [w_in[:, :512], w_in[:, 512:768], w_in[:, 768:]], segs)
            att = windowed_attention(q_proj, kv_proj, p['q_norm_g'][i], p['k_norm_g'][i], p['attn_sink'][i], segs)
            s5_params = (p['s5_lam_re'][i], p['s5_lam_im'][i], p['s5_log_step'][i],
                         p['s5_b_re'][i], p['s5_b_im'][i], p['s5_c_re'][i], p['s5_c_im'][i])
            ys = s5_conv(du, s5_params, segs)
            x = odd_out(x, att, du, ys, p['s5_d'][i], p['s5_glu_w'][i].astype(BF16), p['s5_glu_b'][i],
                        p['od_w_out'][i].astype(BF16), g1, segs)
        x = moe_layer(x, p['norm2_g'][layer].reshape(1, -1), sc2, sh2, g2,
                      p['router_w'][layer], p['router_b'][layer],
                      p['exp_w_gu'][layer].astype(BF16), p['exp_b_gu'][layer].reshape(N_EXPERTS, 1, -1),
                      p['exp_w_down'][layer].astype(BF16), p['exp_b_down'][layer].reshape(N_EXPERTS, 1, -1), segs)
    return x


def kernel(x_prompt, x_sample, c_prompt, c_sample, ada_w, ada_b, norm1_g, norm2_g, ev_w_in, ev_w_out, gla_wa_f, gla_ba_f, gla_wa_b, gla_ba_b, gla_norm_g, conv_w, od_w_in, od_w_out, q_norm_g, k_norm_g, attn_sink, s5_lam_re, s5_lam_im, s5_log_step, s5_b_re, s5_b_im, s5_c_re, s5_c_im, s5_d, s5_glu_w, s5_glu_b, router_w, router_b, exp_w_gu, exp_b_gu, exp_w_down, exp_b_down):
    p = dict(ada_w=ada_w, ada_b=ada_b, norm1_g=norm1_g, norm2_g=norm2_g,
             ev_w_in=ev_w_in, ev_w_out=ev_w_out, gla_wa_f=gla_wa_f, gla_ba_f=gla_ba_f,
             gla_wa_b=gla_wa_b, gla_ba_b=gla_ba_b, gla_norm_g=gla_norm_g, conv_w=conv_w,
             od_w_in=od_w_in, od_w_out=od_w_out, q_norm_g=q_norm_g, k_norm_g=k_norm_g,
             attn_sink=attn_sink, s5_lam_re=s5_lam_re, s5_lam_im=s5_lam_im, s5_log_step=s5_log_step,
             s5_b_re=s5_b_re, s5_b_im=s5_b_im, s5_c_re=s5_c_re, s5_c_im=s5_c_im, s5_d=s5_d,
             s5_glu_w=s5_glu_w, s5_glu_b=s5_glu_b, router_w=router_w, router_b=router_b,
             exp_w_gu=exp_w_gu, exp_b_gu=exp_b_gu, exp_w_down=exp_w_down, exp_b_down=exp_b_down)
    bp, tp, _ = x_prompt.shape
    bs, ts, _ = x_sample.shape
    seg = math.gcd(tp, ts)
    per_p, per_s = tp // seg, ts // seg
    assert bs == 1 and per_p == 1 and per_s == 2, "segment layout: prompt sequences of one segment, one sample sequence of two"
    nseg = bp * per_p + bs * per_s
    segs = Segs(seg=seg, nseg=nseg, join=bp * per_p + 1)
    x = jnp.concatenate([x_prompt.reshape(-1, D_MODEL), x_sample.reshape(-1, D_MODEL)], axis=0)
    c = jnp.concatenate([c_prompt, jnp.repeat(c_sample, per_s, axis=0)], axis=0)
    y = trunk(x, c, p, segs)
    n_p = bp * tp
    return y[:n_p].reshape(x_prompt.shape), y[n_p:].reshape(x_sample.shape)
```
